```python
import math
import jax
import jax.numpy as jnp
from jax import lax
import numpy as np

D_MODEL = 1024
BATCH = 2
SEQ = 16384
DEPTH = 2
DEC_BATCH = 4
DEC_SEQ = 4096
PAST_LEN = 128

GRID_W = 64
QB = 128
ROPE_THETA = 10000.0
EPS = 1e-6
N_BRANCH = 4
BRANCH_W = 512
H_A = 4
DH_A = 64
H_B = 4
Q_LORA = 384
KV_LORA = 256
NOPE_B = 64
ROPE_B = 32
V_B = 128
D_INNER = 512
SSM_HEAD = 64
H_C = D_INNER // SSM_HEAD
N_GROUPS = 2
D_STATE = 64
SSM_CONV = 3
CHUNK = 128
SSM_XBC = D_INNER + 2 * N_GROUPS * D_STATE
H_D = 8
KV_D = 2
DH_D = 64
MEM_LEN = 256
H_MEM = 4
DH_MEM = 128
D_FF = 2816
FFN_CONV = 3

SPLIT_SIZES = (
    H_A * 2 * DH_A, H_A * 2 * DH_A, H_A * 2 * DH_A,
    Q_LORA, KV_LORA, ROPE_B,
    D_INNER, SSM_XBC, H_C, H_C,
    H_D * DH_D, KV_D * DH_D, KV_D * DH_D,
    N_BRANCH * D_MODEL,
)
PROJ_COLS = sum(SPLIT_SIZES)

kernel_name = "hybrid_bidir_gated_parallel_encoder"


def _rms_norm(x, g):
    xf = x.astype(jnp.float32)
    y = xf * lax.rsqrt(jnp.mean(xf * xf, axis=-1, keepdims=True) + EPS)
    return (y * g.astype(jnp.float32)).astype(x.dtype)


def _split_cols(u, sizes):
    parts, start = [], 0
    for s in sizes:
        parts.append(u[..., start:start + s])
        start += s
    return parts


def _rope(x, pos):
    d = x.shape[-1]
    half = d // 2
    freqs = ROPE_THETA ** (-jnp.arange(half, dtype=jnp.float32) / half)
    ang = pos.astype(jnp.float32)[:, None] * freqs[None, :]
    shape = (1, x.shape[1]) + (1,) * (x.ndim - 3) + (half,)
    cos = jnp.cos(ang).reshape(shape)
    sin = jnp.sin(ang).reshape(shape)
    x1 = x[..., :half].astype(jnp.float32)
    x2 = x[..., half:].astype(jnp.float32)
    return jnp.concatenate([x1 * cos - x2 * sin, x2 * cos + x1 * sin], axis=-1).astype(x.dtype)


def _axial_rope(x, row_pos, col_pos):
    half = x.shape[-1] // 2
    return jnp.concatenate([_rope(x[..., :half], row_pos), _rope(x[..., half:], col_pos)], axis=-1)


def _alibi_slopes(n):
    return jnp.array([2.0 ** (-8.0 * (i + 1) / n) for i in range(n)], dtype=jnp.float32)


def _dwconv_centred(x, w, b):
    k_width = w.shape[0]
    pad = k_width // 2
    L = x.shape[1]
    xp = jnp.pad(x, ((0, 0), (pad, pad), (0, 0)))
    out = xp[:, 0:L] * w[0]
    for k in range(1, k_width):
        out = out + xp[:, k:k + L] * w[k]
    return out + b


def _sweep_query_blocks(fn, *qs):
    b, L = qs[0].shape[:2]
    nb = L // QB
    blocks = tuple(jnp.moveaxis(q.reshape((b, nb, QB) + q.shape[2:]), 1, 0) for q in qs)
    starts = jnp.arange(nb, dtype=jnp.int32) * QB
    out = lax.map(lambda args: fn(*args), (starts,) + blocks)
    return jnp.moveaxis(out, 0, 1).reshape((b, L) + out.shape[3:])


def _diff_attention(q, k, v, lam_params, subln_g, layer_idx):
    b, L, _ = q.shape
    q = q.reshape(b, L, H_A, 2, DH_A)
    k = k.reshape(b, L, H_A, 2, DH_A)
    v = v.reshape(b, L, H_A, 2 * DH_A)
    lam_init = 0.8 - 0.6 * math.exp(-0.3 * layer_idx)
    lp = lam_params.astype(jnp.float32)
    lam = jnp.exp(jnp.sum(lp[0] * lp[1])) - jnp.exp(jnp.sum(lp[2] * lp[3])) + lam_init
    slopes = _alibi_slopes(H_A)
    k_pos = jnp.arange(L, dtype=jnp.float32)
    scale = DH_A ** -0.5

    def block(q0, qb):
        q_pos = (q0 + jnp.arange(QB, dtype=jnp.int32)).astype(jnp.float32)
        dist = jnp.abs(q_pos[:, None] - k_pos[None, :])
        logits = jnp.einsum("bqhmd,bkhmd->bhmqk", qb, k).astype(jnp.float32) * scale
        logits = logits - slopes[:, None, None, None] * dist
        p = jax.nn.softmax(logits, axis=-1)
        w = p[:, :, 0] - lam * p[:, :, 1]
        return jnp.einsum("bhqk,bkhe->bqhe", w.astype(v.dtype), v)

    o = _sweep_query_blocks(block, q)
    o = _rms_norm(o, subln_g) * (1.0 - lam_init)
    return o.reshape(b, L, H_A * 2 * DH_A)


def _mla(cq, ckv, kr, q_norm, kv_norm, w_uq, w_ukv, pos):
    b, L, _ = cq.shape
    q = (_rms_norm(cq, q_norm) @ w_uq).reshape(b, L, H_B, NOPE_B + ROPE_B)
    kv = (_rms_norm(ckv, kv_norm) @ w_ukv).reshape(b, L, H_B, NOPE_B + V_B)
    q_nope = q[..., :NOPE_B]
    q_rope = _rope(q[..., NOPE_B:], pos)
    k_nope = kv[..., :NOPE_B]
    v = kv[..., NOPE_B:]
    k_rope = _rope(kr, pos)
    scale = (NOPE_B + ROPE_B) ** -0.5

    def block(q0, qn, qr):
        logits = (jnp.einsum("bqhd,bkhd->bhqk", qn, k_nope)
                  + jnp.einsum("bqhd,bkd->bhqk", qr, k_rope)).astype(jnp.float32) * scale
        p = jax.nn.softmax(logits, axis=-1)
        return jnp.einsum("bhqk,bkhd->bqhd", p.astype(v.dtype), v)

    o = _sweep_query_blocks(block, q_nope, q_rope)
    return o.reshape(b, L, H_B * V_B)


def _ssd_chunked(x, dt, A, B, C):
    b, L, h, p = x.shape
    g, n = B.shape[2], B.shape[3]
    r = h // g
    nc = L // CHUNK
    x = x.astype(jnp.float32).reshape(b, nc, CHUNK, g, r, p)
    dt = dt.astype(jnp.float32).reshape(b, nc, CHUNK, g, r)
    B = B.astype(jnp.float32).reshape(b, nc, CHUNK, g, n)
    C = C.astype(jnp.float32).reshape(b, nc, CHUNK, g, n)
    dA_cs = jnp.cumsum(dt * A.reshape(g, r), axis=2)
    xdt = x * dt[..., None]
    mask = jnp.tril(jnp.ones((CHUNK, CHUNK), dtype=bool))
    seg = jnp.where(mask[:, :, None, None], dA_cs[:, :, :, None] - dA_cs[:, :, None], -jnp.inf)
    decay = jnp.exp(seg)
    cb = jnp.einsum("bcign,bcjgn->bcijg", C, B)
    y_diag = jnp.einsum("bcijg,bcijgr,bcjgrp->bcigrp", cb, decay, xdt)
    decay_to_end = jnp.exp(dA_cs[:, :, -1:] - dA_cs)
    states = jnp.einsum("bcjgn,bcjgr,bcjgrp->bcgrpn", B, decay_to_end, xdt)
    chunk_decay = jnp.exp(dA_cs[:, :, -1])

    def step(s, inp):
        st, dec = inp
        return s * dec[..., None, None] + st, s

    init = jnp.zeros((b, g, r, p, n), jnp.float32)
    _, prev = lax.scan(step, init, (jnp.moveaxis(states, 1, 0), jnp.moveaxis(chunk_decay, 1, 0)))
    prev = jnp.moveaxis(prev, 0, 1)
    y_off = jnp.einsum("bcign,bcigr,bcgrpn->bcigrp", C, jnp.exp(dA_cs), prev)
    return (y_diag + y_off).reshape(b, L, h, p)


def _bidir_ssd_mixer(z, xbc, dt_f, dt_b, conv_w, conv_b, a_log, dt_bias, d_skip, norm_g):
    b, L, _ = z.shape
    xbc = jax.nn.silu(_dwconv_centred(xbc, conv_w, conv_b))
    xs = xbc[..., :D_INNER].reshape(b, L, H_C, SSM_HEAD)
    Bm = xbc[..., D_INNER:D_INNER + N_GROUPS * D_STATE].reshape(b, L, N_GROUPS, D_STATE)
    Cm = xbc[..., D_INNER + N_GROUPS * D_STATE:].reshape(b, L, N_GROUPS, D_STATE)
    A = -jnp.exp(a_log.astype(jnp.float32))
    dtb32 = dt_bias.astype(jnp.float32)
    dtf = jax.nn.softplus(dt_f.astype(jnp.float32) + dtb32[0])
    dtb = jax.nn.softplus(dt_b.astype(jnp.float32) + dtb32[1])
    flip = lambda a: jnp.flip(a, axis=1)
    y_fwd = _ssd_chunked(xs, dtf, A[0], Bm, Cm)
    y_bwd = flip(_ssd_chunked(flip(xs), flip(dtb), A[1], flip(Bm), flip(Cm)))
    y = y_fwd + y_bwd + d_skip.astype(jnp.float32)[:, None] * xs.astype(jnp.float32)
    y = y.reshape(b, L, D_INNER).astype(z.dtype)
    return _rms_norm(y * jax.nn.silu(z), norm_g)


def _axial_gqa(q, k, v, q_norm, k_norm, row_pos, col_pos):
    b, L, _ = q.shape
    q = _rms_norm(q.reshape(b, L, H_D, DH_D), q_norm)
    k = _rms_norm(k.reshape(b, L, KV_D, DH_D), k_norm)
    v = v.reshape(b, L, KV_D, DH_D)
    q = _axial_rope(q, row_pos, col_pos).reshape(b, L, KV_D, H_D // KV_D, DH_D)
    k = _axial_rope(k, row_pos, col_pos)
    scale = DH_D ** -0.5

    def block(q0, qb):
        logits = jnp.einsum("bqgrd,bkgd->bgrqk", qb, k).astype(jnp.float32) * scale
        p = jax.nn.softmax(logits, axis=-1)
        return jnp.einsum("bgrqk,bkgd->bqgrd", p.astype(v.dtype), v)

    o = _sweep_query_blocks(block, q)
    return o.reshape(b, L, H_D * DH_D)


def _parallel_mixer(h, lp, layer_idx):
    b, L, _ = h.shape
    rows = L // GRID_W
    pos = jnp.arange(L, dtype=jnp.int32)
    row_pos = jnp.repeat(jnp.arange(rows, dtype=jnp.int32), GRID_W)
    col_pos = jnp.tile(jnp.arange(GRID_W, dtype=jnp.int32), rows)
    u = h @ lp["w_in"]
    (a_q, a_k, a_v, b_cq, b_ckv, b_kr, c_z, c_xbc, c_dtf, c_dtb,
     d_q, d_k, d_v, g) = _split_cols(u, SPLIT_SIZES)
    y_a = _diff_attention(a_q, a_k, a_v, lp["diff_lambda"], lp["diff_norm"], layer_idx)
    y_b = _mla(b_cq, b_ckv, b_kr, lp["mla_q_norm"], lp["mla_kv_norm"], lp["w_mla_uq"], lp["w_mla_ukv"], pos)
    y_c = _bidir_ssd_mixer(c_z, c_xbc, c_dtf, c_dtb, lp["ssm_conv_w"], lp["ssm_conv_b"], lp["ssm_a_log"],
                           lp["ssm_dt_bias"], lp["ssm_d"], lp["ssm_norm"])
    y_d = _axial_gqa(d_q, d_k, d_v, lp["gqa_q_norm"], lp["gqa_k_norm"], row_pos, col_pos)
    ys = jnp.stack([y_a, y_b, y_c, y_d], axis=2)
    proj = jnp.einsum("blnc,ncd->blnd", ys, lp["w_branch"])
    gates = jax.nn.sigmoid(g.reshape(b, L, N_BRANCH, D_MODEL))
    merged = jnp.sum(proj * gates, axis=2)
    return merged @ lp["w_out"]


def _mem_attention(h, mem, mem_norm, w_q, w_kv, w_o):
    b, L, _ = h.shape
    m = _rms_norm(mem, mem_norm)
    q = (h @ w_q).reshape(b, L, H_MEM, DH_MEM)
    kv = (m @ w_kv).reshape(b, mem.shape[1], H_MEM, 2 * DH_MEM)
    k, v = kv[..., :DH_MEM], kv[..., DH_MEM:]
    logits = jnp.einsum("blhd,bmhd->bhlm", q, k).astype(jnp.float32) * (DH_MEM ** -0.5)
    p = jax.nn.softmax(logits, axis=-1)
    o = jnp.einsum("bhlm,bmhd->blhd", p.astype(v.dtype), v)
    return o.reshape(b, L, H_MEM * DH_MEM) @ w_o


def _conv_ffn(h, w_in, conv_w, conv_b, w_out):
    u = _dwconv_centred(h @ w_in, conv_w, conv_b)
    a, g = u[..., :D_FF], u[..., D_FF:]
    return (jax.nn.gelu(a, approximate=False) * g) @ w_out


def _layer(x, mem, lp, layer_idx):
    pre, post = lp["norm_pre"], lp["norm_post"]
    x = x + _rms_norm(_parallel_mixer(_rms_norm(x, pre[0]), lp, layer_idx), post[0])
    x = x + _rms_norm(_mem_attention(_rms_norm(x, pre[1]), mem, lp["mem_norm"], lp["w_mem_q"],
                                     lp["w_mem_kv"], lp["w_mem_o"]), post[1])
    x = x + _rms_norm(_conv_ffn(_rms_norm(x, pre[2]), lp["w_ffn_in"], lp["ffn_conv_w"],
                                lp["ffn_conv_b"], lp["w_ffn_out"]), post[2])
    return x


def _trunk(x, mem, params):
    for l in range(DEPTH):
        lp = {name: arr[l] for name, arr in params.items()}
        x = _layer(x, mem, lp, l)
    return x


def setup_inputs(seed: int = 0) -> dict:
    key = jax.random.key(seed)
    it = iter(jax.random.split(key, 40))
    f32 = jnp.float32

    def dense(shape, fan_in):
        return jax.random.normal(next(it), shape, f32) * (fan_in ** -0.5)

    def gain(shape):
        return 1.0 + 0.02 * jax.random.normal(next(it), shape, f32)

    def small(shape):
        return 0.02 * jax.random.normal(next(it), shape, f32)

    x_prompt = jax.random.normal(next(it), (BATCH, SEQ, D_MODEL), f32)
    x_sample = jax.random.normal(next(it), (DEC_BATCH, DEC_SEQ, D_MODEL), f32)
    mem_prompt = jax.random.normal(next(it), (BATCH, MEM_LEN, D_MODEL), f32)
    mem_sample = jax.random.normal(next(it), (DEC_BATCH, MEM_LEN, D_MODEL), f32)
    w_in = dense((DEPTH, D_MODEL, PROJ_COLS), D_MODEL)
    w_branch = dense((DEPTH, N_BRANCH, BRANCH_W, D_MODEL), BRANCH_W)
    w_out = dense((DEPTH, D_MODEL, D_MODEL), D_MODEL)
    diff_lambda = 0.1 * jax.random.normal(next(it), (DEPTH, 4, DH_A), f32)
    diff_norm = gain((DEPTH, 2 * DH_A))
    mla_q_norm = gain((DEPTH, Q_LORA))
    mla_kv_norm = gain((DEPTH, KV_LORA))
    w_mla_uq = dense((DEPTH, Q_LORA, H_B * (NOPE_B + ROPE_B)), Q_LORA)
    w_mla_ukv = dense((DEPTH, KV_LORA, H_B * (NOPE_B + V_B)), KV_LORA)
    ssm_conv_w = dense((DEPTH, SSM_CONV, SSM_XBC), SSM_CONV)
    ssm_conv_b = small((DEPTH, SSM_XBC))
    ssm_a_log = jnp.log(jax.random.uniform(next(it), (DEPTH, 2, H_C), f32, 1.0, 16.0))
    dt0 = jnp.exp(jax.random.uniform(next(it), (DEPTH, 2, H_C), f32, math.log(1e-3), math.log(1e-1)))
    ssm_dt_bias = dt0 + jnp.log(-jnp.expm1(-dt0))
    ssm_d = gain((DEPTH, H_C))
    ssm_norm = gain((DEPTH, D_INNER))
    gqa_q_norm = gain((DEPTH, DH_D))
    gqa_k_norm = gain((DEPTH, DH_D))
    mem_norm = gain((DEPTH, D_MODEL))
    w_mem_q = dense((DEPTH, D_MODEL, H_MEM * DH_MEM), D_MODEL)
    w_mem_kv = dense((DEPTH, D_MODEL, 2 * H_MEM * DH_MEM), D_MODEL)
    w_mem_o = dense((DEPTH, H_MEM * DH_MEM, D_MODEL), H_MEM * DH_MEM)
    w_ffn_in = dense((DEPTH, D_MODEL, 2 * D_FF), D_MODEL)
    ffn_conv_w = dense((DEPTH, FFN_CONV, 2 * D_FF), FFN_CONV)
    ffn_conv_b = small((DEPTH, 2 * D_FF))
    w_ffn_out = dense((DEPTH, D_FF, D_MODEL), D_FF)
    norm_pre = gain((DEPTH, 3, D_MODEL))
    norm_post = gain((DEPTH, 3, D_MODEL))
    return {
        "x_prompt": x_prompt, "x_sample": x_sample, "mem_prompt": mem_prompt, "mem_sample": mem_sample,
        "w_in": w_in, "w_branch": w_branch, "w_out": w_out,
        "diff_lambda": diff_lambda, "diff_norm": diff_norm,
        "mla_q_norm": mla_q_norm, "mla_kv_norm": mla_kv_norm, "w_mla_uq": w_mla_uq, "w_mla_ukv": w_mla_ukv,
        "ssm_conv_w": ssm_conv_w, "ssm_conv_b": ssm_conv_b, "ssm_a_log": ssm_a_log,
        "ssm_dt_bias": ssm_dt_bias, "ssm_d": ssm_d, "ssm_norm": ssm_norm,
        "gqa_q_norm": gqa_q_norm, "gqa_k_norm": gqa_k_norm,
        "mem_norm": mem_norm, "w_mem_q": w_mem_q, "w_mem_kv": w_mem_kv, "w_mem_o": w_mem_o,
        "w_ffn_in": w_ffn_in, "ffn_conv_w": ffn_conv_w, "ffn_conv_b": ffn_conv_b, "w_ffn_out": w_ffn_out,
        "norm_pre": norm_pre, "norm_post": norm_post,
    }


def reference(x_prompt, x_sample, mem_prompt, mem_sample, w_in, w_branch, w_out, diff_lambda, diff_norm,
              mla_q_norm, mla_kv_norm, w_mla_uq, w_mla_ukv, ssm_conv_w, ssm_conv_b, ssm_a_log,
              ssm_dt_bias, ssm_d, ssm_norm, gqa_q_norm, gqa_k_norm, mem_norm, w_mem_q, w_mem_kv, w_mem_o,
              w_ffn_in, ffn_conv_w, ffn_conv_b, w_ffn_out, norm_pre, norm_post):
    params = {
        "w_in": w_in, "w_branch": w_branch, "w_out": w_out,
        "diff_lambda": diff_lambda, "diff_norm": diff_norm,
        "mla_q_norm": mla_q_norm, "mla_kv_norm": mla_kv_norm, "w_mla_uq": w_mla_uq, "w_mla_ukv": w_mla_ukv,
        "ssm_conv_w": ssm_conv_w, "ssm_conv_b": ssm_conv_b, "ssm_a_log": ssm_a_log,
        "ssm_dt_bias": ssm_dt_bias, "ssm_d": ssm_d, "ssm_norm": ssm_norm,
        "gqa_q_norm": gqa_q_norm, "gqa_k_norm": gqa_k_norm,
        "mem_norm": mem_norm, "w_mem_q": w_mem_q, "w_mem_kv": w_mem_kv, "w_mem_o": w_mem_o,
        "w_ffn_in": w_ffn_in, "ffn_conv_w": ffn_conv_w, "ffn_conv_b": ffn_conv_b, "w_ffn_out": w_ffn_out,
        "norm_pre": norm_pre, "norm_post": norm_post,
    }
    y_prompt = _trunk(x_prompt, mem_prompt, params)
    y_sample = _trunk(x_sample, mem_sample, params)
    return (y_prompt, y_sample)
```

```python
import functools
import math

import jax
import jax.numpy as jnp
import numpy as np
from jax import lax
from jax.experimental import pallas as pl
from jax.experimental.pallas import tpu as pltpu

F32 = jnp.float32
BF16 = jnp.bfloat16
HIGHEST = lax.Precision.HIGHEST

D_MODEL = 1024
DEPTH = 2
GRID_W = 64
ROPE_THETA = 10000.0
EPS = 1e-6
N_BRANCH = 4
BRANCH_W = 512
H_A = 4
DH_A = 64
H_B = 4
Q_LORA = 384
KV_LORA = 256
NOPE_B = 64
ROPE_B = 32
V_B = 128
D_INNER = 512
SSM_HEAD = 64
H_C = D_INNER // SSM_HEAD
N_GROUPS = 2
D_STATE = 64
CHUNK = 128
H_D = 8
KV_D = 2
DH_D = 64
H_MEM = 4
DH_MEM = 128
D_FF = 2816

LANE = 128
LOG2E = 1.4426950408889634
NEG_BIG = -1e30
ONES_ROWS = 16
VMEM_LIMIT = 56 * 1024 * 1024

COL_G = 0
COL_AQ = 32
COL_AK = 36
COL_AV = 40
COL_CZ = 44
COL_CXS = 48
COL_DQ = 52
COL_DQR = 56
COL_BCQ = 60
COL_BKR = 63
COL_BCKV = 64
COL_CBC = 66
COL_BKRR = 68
COL_CDT = 69
COL_DK = 70
COL_DKR = 71
COL_DV = 72
N_COLBLK = 76
PROJ_TN = 512


def _params(sem):
    return pltpu.CompilerParams(dimension_semantics=sem, vmem_limit_bytes=VMEM_LIMIT)


def _rms(x, g):
    return x * lax.rsqrt(jnp.mean(x * x, axis=-1, keepdims=True) + EPS) * g


def _norm_matmul_kernel(x_ref, g_ref, w_ref, o_ref, xn_ref):
    @pl.when(pl.program_id(1) == 0)
    def _():
        xn_ref[...] = _rms(x_ref[...], g_ref[...]).astype(BF16)

    o_ref[...] = jnp.dot(xn_ref[...], w_ref[...], preferred_element_type=F32).astype(o_ref.dtype)


def _norm_matmul(x, g, w, out_dtype, tm, tn):
    t, d = x.shape
    n = w.shape[1]
    return pl.pallas_call(
        _norm_matmul_kernel,
        grid=(t // tm, n // tn),
        in_specs=[
            pl.BlockSpec((tm, d), lambda i, j: (i, 0)),
            pl.BlockSpec((1, d), lambda i, j: (0, 0)),
            pl.BlockSpec((d, tn), lambda i, j: (0, j)),
        ],
        out_specs=pl.BlockSpec((tm, tn), lambda i, j: (i, j)),
        out_shape=jax.ShapeDtypeStruct((t, n), out_dtype),
        scratch_shapes=[pltpu.VMEM((tm, d), BF16)],
        compiler_params=_params(("parallel", "arbitrary")),
        name="norm_matmul",
    )(x, g.reshape(1, d), w)


def _ones_rows(tm):
    row = lax.broadcasted_iota(jnp.int32, (ONES_ROWS, tm), 0)
    return jnp.where(row == 0, 1.0, 0.0).astype(BF16)


def _prep_a_kernel(q_ref, k_ref, v_ref, qt_ref, ko_ref, vt_ref):
    q = q_ref[0]
    tm = q.shape[0]
    lane = lax.broadcasted_iota(jnp.int32, q.shape, 1)
    q1 = jnp.where(lane < DH_A, q, 0.0)
    qt_ref[0, 0] = q1.T.astype(BF16)
    qt_ref[0, 1] = (q - q1).T.astype(BF16)
    ko_ref[0, 0] = k_ref[0].astype(BF16)
    vt_ref[0, 0, 0:2 * DH_A, :] = v_ref[0].T.astype(BF16)
    vt_ref[0, 0, 2 * DH_A:, :] = _ones_rows(tm)


def _prep_a(u, b, l, tm):
    dv = 2 * DH_A + ONES_ROWS
    return pl.pallas_call(
        _prep_a_kernel,
        grid=(b, l // tm, H_A),
        in_specs=[
            pl.BlockSpec((1, tm, LANE), lambda bi, i, h: (bi, i, COL_AQ + h)),
            pl.BlockSpec((1, tm, LANE), lambda bi, i, h: (bi, i, COL_AK + h)),
            pl.BlockSpec((1, tm, LANE), lambda bi, i, h: (bi, i, COL_AV + h)),
        ],
        out_specs=[
            pl.BlockSpec((1, 2, LANE, tm), lambda bi, i, h: (bi, h, 0, i)),
            pl.BlockSpec((1, 1, tm, LANE), lambda bi, i, h: (bi, h, i, 0)),
            pl.BlockSpec((1, 1, dv, tm), lambda bi, i, h: (bi, h, 0, i)),
        ],
        out_shape=[
            jax.ShapeDtypeStruct((b, 2 * H_A, LANE, l), BF16),
            jax.ShapeDtypeStruct((b, H_A, l, LANE), BF16),
            jax.ShapeDtypeStruct((b, H_A, dv, l), BF16),
        ],
        compiler_params=_params(("parallel", "parallel", "parallel")),
        name="prep_diff",
    )(u, u, u)


def _prep_b_kernel(cq_ref, ckv_ref, kr_ref, krr_ref, cos_ref, sin_ref, gq_ref, gkv_ref,
                   wq_ref, wqr_ref, wk_ref, wv_ref, qt_ref, ko_ref, vt_ref):
    tm = cq_ref.shape[1]
    cos = cos_ref[...]
    sin = sin_ref[...]
    cqn = _rms(cq_ref[0], gq_ref[...]).astype(BF16)
    ckn = _rms(ckv_ref[0], gkv_ref[...]).astype(BF16)
    qa = jnp.dot(cqn, wq_ref[...], preferred_element_type=F32)
    qb = jnp.dot(cqn, wqr_ref[...], preferred_element_type=F32)
    kn = jnp.dot(ckn, wk_ref[...], preferred_element_type=F32)
    vv = jnp.dot(ckn, wv_ref[...], preferred_element_type=F32)
    krope = kr_ref[0] * cos + krr_ref[0] * sin
    ones = _ones_rows(tm)
    for h in range(H_B):
        sl = slice(h * LANE, (h + 1) * LANE)
        qh = qa[:, sl] * cos + qb[:, sl] * sin
        qt_ref[0, h] = qh.T.astype(BF16)
        ko_ref[0, h] = (kn[:, sl] + krope).astype(BF16)
        vt_ref[0, h, 0:V_B, :] = vv[:, sl].T.astype(BF16)
        vt_ref[0, h, V_B:, :] = ones


def _prep_b(u, b, l, tm, cos, sin, gq, gkv, wq, wqr, wk, wv):
    dv = V_B + ONES_ROWS
    const = lambda shape: pl.BlockSpec(shape, lambda bi, i: (0,) * len(shape))
    return pl.pallas_call(
        _prep_b_kernel,
        grid=(b, l // tm),
        in_specs=[
            pl.BlockSpec((1, tm, Q_LORA), lambda bi, i: (bi, i, COL_BCQ // 3)),
            pl.BlockSpec((1, tm, KV_LORA), lambda bi, i: (bi, i, COL_BCKV // 2)),
            pl.BlockSpec((1, tm, LANE), lambda bi, i: (bi, i, COL_BKR)),
            pl.BlockSpec((1, tm, LANE), lambda bi, i: (bi, i, COL_BKRR)),
            pl.BlockSpec((tm, LANE), lambda bi, i: (i, 0)),
            pl.BlockSpec((tm, LANE), lambda bi, i: (i, 0)),
            const((1, Q_LORA)), const((1, KV_LORA)),
            const((Q_LORA, H_B * LANE)), const((Q_LORA, H_B * LANE)),
            const((KV_LORA, H_B * LANE)), const((KV_LORA, H_B * LANE)),
        ],
        out_specs=[
            pl.BlockSpec((1, H_B, LANE, tm), lambda bi, i: (bi, 0, 0, i)),
            pl.BlockSpec((1, H_B, tm, LANE), lambda bi, i: (bi, 0, i, 0)),
            pl.BlockSpec((1, H_B, dv, tm), lambda bi, i: (bi, 0, 0, i)),
        ],
        out_shape=[
            jax.ShapeDtypeStruct((b, H_B, LANE, l), BF16),
            jax.ShapeDtypeStruct((b, H_B, l, LANE), BF16),
            jax.ShapeDtypeStruct((b, H_B, dv, l), BF16),
        ],
        compiler_params=_params(("parallel", "parallel")),
        name="prep_mla",
    )(u, u, u, u, cos, sin, gq, gkv, wq, wqr, wk, wv)


def _prep_d_kernel(q_ref, qr_ref, k_ref, kr_ref, v_ref, cos_ref, sin_ref, gq_ref, gqr_ref, gk_ref, gkr_ref,
                   qt_ref, ko_ref, vt_ref, *, qscale):
    tm = q_ref.shape[1]
    cos = cos_ref[...]
    sin = sin_ref[...]
    r_i = lax.broadcasted_iota(jnp.int32, (LANE, LANE), 0) // DH_D
    c_i = lax.broadcasted_iota(jnp.int32, (LANE, LANE), 1) // DH_D
    blockdiag = jnp.where(r_i == c_i, 1.0, 0.0).astype(F32)

    def norm_rope(x, xr, g, gr):
        ss = jnp.dot(x * x, blockdiag, precision=HIGHEST, preferred_element_type=F32)
        r = lax.rsqrt(ss * (1.0 / DH_D) + EPS)
        return r * (x * g * cos + xr * gr * sin)

    q = q_ref[0]
    qr = qr_ref[0]
    zeros = jnp.zeros((DH_D, tm), F32)
    for j in range(H_D // 2):
        sl = slice(j * LANE, (j + 1) * LANE)
        y = norm_rope(q[:, sl], qr[:, sl], gq_ref[...], gqr_ref[...]) * qscale
        yt = y.T
        grp = (2 * j) // (H_D // KV_D)
        for e in range(2):
            piece = yt[e * DH_D:(e + 1) * DH_D]
            full = jnp.concatenate([piece, zeros], axis=0) if grp == 0 else jnp.concatenate([zeros, piece], axis=0)
            qt_ref[0, 2 * j + e] = full.astype(BF16)
    ko_ref[0, 0] = norm_rope(k_ref[0], kr_ref[0], gk_ref[...], gkr_ref[...]).astype(BF16)
    vt = v_ref[0].T
    ones = _ones_rows(tm)
    for grp in range(KV_D):
        vt_ref[0, grp, 0:DH_D, :] = vt[grp * DH_D:(grp + 1) * DH_D].astype(BF16)
        vt_ref[0, grp, DH_D:, :] = ones


def _prep_d(u, b, l, tm, cos, sin, gq, gqr, gk, gkr):
    dv = DH_D + ONES_ROWS
    const = lambda shape: pl.BlockSpec(shape, lambda bi, i: (0,) * len(shape))
    return pl.pallas_call(
        functools.partial(_prep_d_kernel, qscale=DH_D ** -0.5 * LOG2E),
        grid=(b, l // tm),
        in_specs=[
            pl.BlockSpec((1, tm, 4 * LANE), lambda bi, i: (bi, i, COL_DQ // 4)),
            pl.BlockSpec((1, tm, 4 * LANE), lambda bi, i: (bi, i, COL_DQR // 4)),
            pl.BlockSpec((1, tm, LANE), lambda bi, i: (bi, i, COL_DK)),
            pl.BlockSpec((1, tm, LANE), lambda bi, i: (bi, i, COL_DKR)),
            pl.BlockSpec((1, tm, LANE), lambda bi, i: (bi, i, COL_DV)),
            pl.BlockSpec((tm, LANE), lambda bi, i: (i, 0)),
            pl.BlockSpec((tm, LANE), lambda bi, i: (i, 0)),
            const((1, LANE)), const((1, LANE)), const((1, LANE)), const((1, LANE)),
        ],
        out_specs=[
            pl.BlockSpec((1, H_D, LANE, tm), lambda bi, i: (bi, 0, 0, i)),
            pl.BlockSpec((1, 1, tm, LANE), lambda bi, i: (bi, 0, i, 0)),
            pl.BlockSpec((1, KV_D, dv, tm), lambda bi, i: (bi, 0, 0, i)),
        ],
        out_shape=[
            jax.ShapeDtypeStruct((b, H_D, LANE, l), BF16),
            jax.ShapeDtypeStruct((b, 1, l, LANE), BF16),
            jax.ShapeDtypeStruct((b, KV_D, dv, l), BF16),
        ],
        compiler_params=_params(("parallel", "parallel")),
        name="prep_gqa",
    )(u, u, u, u, u, cos, sin, gq, gqr, gk, gkr)


def _flash_kernel(slope_ref, qt_ref, k_ref, vt_ref, o_ref, m_ref, acc_ref, *, tk, dv, alibi):
    tq = qt_ref.shape[3]
    l = k_ref.shape[2]
    qt = qt_ref[0, 0]
    m_ref[...] = jnp.full(m_ref.shape, NEG_BIG, F32)
    acc_ref[...] = jnp.zeros(acc_ref.shape, F32)
    if alibi:
        slope = slope_ref[pl.program_id(1)]
        q0 = pl.program_id(2) * tq
        rel = (lax.broadcasted_iota(jnp.int32, (tk, tq), 0) - lax.broadcasted_iota(jnp.int32, (tk, tq), 1))

    def body(kt, carry):
        k0 = pl.multiple_of(kt * tk, tk)
        s = jnp.dot(k_ref[0, 0, pl.ds(k0, tk), :], qt, preferred_element_type=F32)
        if alibi:
            s = s - slope * jnp.abs(rel + (k0 - q0)).astype(F32)
        m_old = m_ref[...]
        m_new = jnp.maximum(m_old, jnp.max(s, axis=0, keepdims=True))
        alpha = jnp.exp2(m_old - m_new)
        p = jnp.exp2(s - m_new).astype(BF16)
        pv = jnp.dot(vt_ref[0, 0, :, pl.ds(k0, tk)], p, preferred_element_type=F32)
        acc_ref[...] = alpha * acc_ref[...] + pv
        m_ref[...] = m_new
        return carry

    lax.fori_loop(0, l // tk, body, 0)
    acc = acc_ref[...]
    o_ref[0, 0] = acc[0:dv] / acc[dv:dv + 1]


def _flash(qt, k, vt, slopes, *, tq, tk, alibi):
    b, hq, _, l = qt.shape
    hk = k.shape[1]
    hv = vt.shape[1]
    dvp = vt.shape[2]
    dv = dvp - ONES_ROWS
    return pl.pallas_call(
        functools.partial(_flash_kernel, tk=tk, dv=dv, alibi=alibi),
        grid=(b, hq, l // tq),
        in_specs=[
            pl.BlockSpec(memory_space=pltpu.SMEM),
            pl.BlockSpec((1, 1, LANE, tq), lambda bi, h, i: (bi, h, 0, i)),
            pl.BlockSpec((1, 1, l, LANE), lambda bi, h, i: (bi, h // (hq // hk), 0, 0)),
            pl.BlockSpec((1, 1, dvp, l), lambda bi, h, i: (bi, h // (hq // hv), 0, 0)),
        ],
        out_specs=pl.BlockSpec((1, 1, dv, tq), lambda bi, h, i: (bi, h, 0, i)),
        out_shape=jax.ShapeDtypeStruct((b, hq, dv, l), F32),
        scratch_shapes=[pltpu.VMEM((1, tq), F32), pltpu.VMEM((dvp, tq), F32)],
        compiler_params=_params(("parallel", "parallel", "parallel")),
        name="flash_" + ("alibi" if alibi else "plain") + f"_{hq}_{dv}",
    )(slopes, qt, k, vt)


def _conv3_rows(xe, w_ref, b_ref, tm):
    n = xe.shape[0]
    up = pltpu.roll(xe, 1, 0)
    dn = pltpu.roll(xe, n - 1, 0)
    y = up * w_ref[0:1, :] + xe * w_ref[1:2, :] + dn * w_ref[2:3, :] + b_ref[...]
    return y[8:8 + tm]


def _halo_rows(prev_ref, main_ref, next_ref, i, n_i):
    prev = jnp.where(i == 0, 0.0, prev_ref[0])
    nxt = jnp.where(i == n_i - 1, 0.0, next_ref[0])
    return jnp.concatenate([prev, main_ref[0], nxt], axis=0)


def _conv_silu_kernel(p_ref, x_ref, n_ref, w_ref, b_ref, o_ref):
    tm = x_ref.shape[1]
    xe = _halo_rows(p_ref, x_ref, n_ref, pl.program_id(1), pl.num_programs(1))
    y = _conv3_rows(xe, w_ref, b_ref, tm)
    o_ref[0] = y * (1.0 / (1.0 + jnp.exp(-y)))


def _conv_silu(u, b, l, tm, col_blk, width, w, bias):
    r8 = tm // 8
    n8 = l // 8
    return pl.pallas_call(
        _conv_silu_kernel,
        grid=(b, l // tm),
        in_specs=[
            pl.BlockSpec((1, 8, width), lambda bi, i: (bi, jnp.maximum(i * r8 - 1, 0), col_blk)),
            pl.BlockSpec((1, tm, width), lambda bi, i: (bi, i, col_blk)),
            pl.BlockSpec((1, 8, width), lambda bi, i: (bi, jnp.minimum((i + 1) * r8, n8 - 1), col_blk)),
            pl.BlockSpec((3, width), lambda bi, i: (0, 0)),
            pl.BlockSpec((1, width), lambda bi, i: (0, 0)),
        ],
        out_specs=pl.BlockSpec((1, tm, width), lambda bi, i: (bi, i, 0)),
        out_shape=jax.ShapeDtypeStruct((b, l, width), F32),
        compiler_params=_params(("parallel", "parallel")),
        name=f"conv_silu_{width}",
    )(u, u, u, w, bias)


def _ssd_kernel(*refs, reverse, final):
    if final:
        (xs_ref, bc_b_ref, bc_c_ref, dt_ref, alog_ref, dtb_ref, yprev_ref, z_ref, dskip_ref, norm_ref,
         y_ref, state_ref) = refs
    else:
        xs_ref, bc_b_ref, bc_c_ref, dt_ref, alog_ref, dtb_ref, y_ref, state_ref = refs
    q = CHUNK

    @pl.when(pl.program_id(1) == 0)
    def _():
        state_ref[...] = jnp.zeros(state_ref.shape, F32)

    row = lax.broadcasted_iota(jnp.int32, (q, q), 0)
    col = lax.broadcasted_iota(jnp.int32, (q, q), 1)
    mask = (col >= row) if reverse else (col <= row)
    tri = jnp.where(mask, 1.0, 0.0).astype(F32)
    head_off = H_C if reverse else 0

    x = dt_ref[0] + dtb_ref[...]
    dt = jnp.maximum(x, 0.0) + jnp.log(1.0 + jnp.exp(-jnp.abs(x)))
    a = dt * (-jnp.exp(alog_ref[...]))
    cs = jnp.dot(tri, a, precision=HIGHEST, preferred_element_type=F32)
    e_r = lax.broadcasted_iota(jnp.int32, (q, D_INNER), 0)
    e_c = lax.broadcasted_iota(jnp.int32, (q, D_INNER), 1) // SSM_HEAD + head_off
    expand = jnp.where(e_r == e_c, 1.0, 0.0).astype(F32)
    dt_x = jnp.dot(dt, expand, precision=HIGHEST, preferred_element_type=F32)
    cs_x = jnp.dot(cs, expand, precision=HIGHEST, preferred_element_type=F32)
    tot_x = cs_x[0:1] if reverse else cs_x[q - 1:q]

    xs = xs_ref[0]
    xdt = xs * dt_x
    w_in = (xdt * jnp.exp(tot_x - cs_x)).astype(BF16)
    bt = bc_b_ref[0].T
    bt16 = bt.astype(BF16)
    cm = bc_c_ref[0]
    cs_t = cs.T
    lane = lax.broadcasted_iota(jnp.int32, (q, LANE), 1)
    rowi = lax.broadcasted_iota(jnp.int32, (q, LANE), 0)

    outs = []
    for pair in range(H_C // 2):
        grp = (2 * pair) // (H_C // N_GROUPS)
        in_grp = (lane // D_STATE) == grp
        c_g = jnp.where(in_grp, cm, 0.0).astype(BF16)
        cb = jnp.dot(c_g, bt16, preferred_element_type=F32)
        psl = slice(pair * LANE, (pair + 1) * LANE)
        x_p = xdt[:, psl].astype(BF16)
        ys = []
        for e in range(2):
            hl = head_off + 2 * pair + e
            seg = cs[:, hl:hl + 1] - cs_t[hl:hl + 1, :]
            dec = jnp.exp(jnp.where(mask, seg, -jnp.inf))
            ys.append(jnp.dot((cb * dec).astype(BF16), x_p, preferred_element_type=F32))
        y_diag = jnp.where(lane < SSM_HEAD, ys[0], ys[1])
        st = state_ref[pair]
        y_off = jnp.dot(c_g, st.astype(BF16), preferred_element_type=F32) * jnp.exp(cs_x[:, psl])
        outs.append(y_diag + y_off)
        upd = jnp.dot(bt16, w_in[:, psl], preferred_element_type=F32)
        keep = (rowi // D_STATE) == grp
        state_ref[pair] = jnp.where(keep, st * jnp.exp(tot_x[:, psl]) + upd, 0.0)
    y = jnp.concatenate(outs, axis=1)
    if final:
        y = y + yprev_ref[0] + dskip_ref[...] * xs
        z = z_ref[0]
        y = y * (z * (1.0 / (1.0 + jnp.exp(-z))))
        y = _rms(y, norm_ref[...])
    y_ref[0] = y


def _ssd_scan(xs, bc, u, b, l, alog, dtb, reverse, final_args=None):
    nc = l // CHUNK
    cidx = (lambda c: nc - 1 - c) if reverse else (lambda c: c)
    in_specs = [
        pl.BlockSpec((1, CHUNK, D_INNER), lambda bi, c: (bi, cidx(c), 0)),
        pl.BlockSpec((1, CHUNK, LANE), lambda bi, c: (bi, cidx(c), 0)),
        pl.BlockSpec((1, CHUNK, LANE), lambda bi, c: (bi, cidx(c), 1)),
        pl.BlockSpec((1, CHUNK, LANE), lambda bi, c: (bi, cidx(c), COL_CDT)),
        pl.BlockSpec((1, LANE), lambda bi, c: (0, 0)),
        pl.BlockSpec((1, LANE), lambda bi, c: (0, 0)),
    ]
    args = [xs, bc, bc, u, alog, dtb]
    final = final_args is not None
    if final:
        yprev, dskip, norm = final_args
        in_specs += [
            pl.BlockSpec((1, CHUNK, D_INNER), lambda bi, c: (bi, cidx(c), 0)),
            pl.BlockSpec((1, CHUNK, D_INNER), lambda bi, c: (bi, cidx(c), COL_CZ // 4)),
            pl.BlockSpec((1, D_INNER), lambda bi, c: (0, 0)),
            pl.BlockSpec((1, D_INNER), lambda bi, c: (0, 0)),
        ]
        args += [yprev, u, dskip, norm]
    return pl.pallas_call(
        functools.partial(_ssd_kernel, reverse=reverse, final=final),
        grid=(b, nc),
        in_specs=in_specs,
        out_specs=pl.BlockSpec((1, CHUNK, D_INNER), lambda bi, c: (bi, cidx(c), 0)),
        out_shape=jax.ShapeDtypeStruct((b, l, D_INNER), F32),
        scratch_shapes=[pltpu.VMEM((H_C // 2, LANE, LANE), F32)],
        compiler_params=_params(("parallel", "arbitrary")),
        name="ssd_bwd" if reverse else "ssd_fwd",
    )(*args)


def _merge_kernel(x_ref, oa_ref, ob_ref, yc_ref, od_ref, g_ref, lam_ref, subln_ref, wb_ref, wo_ref, post_ref,
                  o_ref, *, lam_init):
    lp = lam_ref[...]
    lam = (jnp.exp(jnp.sum(lp[0:1] * lp[1:2], axis=-1, keepdims=True))
           - jnp.exp(jnp.sum(lp[2:3] * lp[3:4], axis=-1, keepdims=True)) + lam_init)
    ya = []
    for h in range(H_A):
        o = (oa_ref[0, 2 * h] - lam * oa_ref[0, 2 * h + 1]).T
        ya.append(_rms(o, subln_ref[...]) * (1.0 - lam_init))
    ya = jnp.concatenate(ya, axis=1).astype(BF16)
    yb = jnp.concatenate([ob_ref[0, h].T for h in range(H_B)], axis=1).astype(BF16)
    yc = yc_ref[0].astype(BF16)
    yd = jnp.concatenate([jnp.concatenate([od_ref[0, 2 * j], od_ref[0, 2 * j + 1]], axis=0).T
                          for j in range(H_D // 2)], axis=1).astype(BF16)
    merged = None
    for n, y in enumerate((ya, yb, yc, yd)):
        proj = jnp.dot(y, wb_ref[n], preferred_element_type=F32)
        gn = g_ref[0, :, n * D_MODEL:(n + 1) * D_MODEL]
        term = proj * (1.0 / (1.0 + jnp.exp(-gn)))
        merged = term if merged is None else merged + term
    out = jnp.dot(merged.astype(BF16), wo_ref[...], preferred_element_type=F32)
    o_ref[0] = x_ref[0] + _rms(out, post_ref[...])


def _merge(x, oa, ob, yc, od, u, lam_p, subln, wb, wo, post, lam_init, tm):
    b, l, d = x.shape
    const = lambda shape: pl.BlockSpec(shape, lambda bi, i: (0,) * len(shape))
    return pl.pallas_call(
        functools.partial(_merge_kernel, lam_init=lam_init),
        grid=(b, l // tm),
        in_specs=[
            pl.BlockSpec((1, tm, d), lambda bi, i: (bi, i, 0)),
            pl.BlockSpec((1, 2 * H_A, LANE, tm), lambda bi, i: (bi, 0, 0, i)),
            pl.BlockSpec((1, H_B, V_B, tm), lambda bi, i: (bi, 0, 0, i)),
            pl.BlockSpec((1, tm, D_INNER), lambda bi, i: (bi, i, 0)),
            pl.BlockSpec((1, H_D, DH_D, tm), lambda bi, i: (bi, 0, 0, i)),
            pl.BlockSpec((1, tm, N_BRANCH * d), lambda bi, i: (bi, i, COL_G // 32)),
            const((4, DH_A)), const((1, 2 * DH_A)),
            const((N_BRANCH, BRANCH_W, d)), const((d, d)), const((1, d)),
        ],
        out_specs=pl.BlockSpec((1, tm, d), lambda bi, i: (bi, i, 0)),
        out_shape=jax.ShapeDtypeStruct((b, l, d), F32),
        compiler_params=_params(("parallel", "parallel")),
        name="merge",
    )(x, oa, ob, yc, od, u, lam_p, subln, wb, wo, post)


def _mem_attn_kernel(x_ref, kv_ref, pre_ref, wq_ref, wo_ref, post_ref, o_ref):
    x = x_ref[0]
    xn = _rms(x, pre_ref[...]).astype(BF16)
    q = jnp.dot(xn, wq_ref[...], preferred_element_type=F32).astype(BF16)
    outs = []
    for h in range(H_MEM):
        k = kv_ref[0, :, 2 * h * DH_MEM:(2 * h + 1) * DH_MEM]
        v = kv_ref[0, :, (2 * h + 1) * DH_MEM:(2 * h + 2) * DH_MEM]
        s = lax.dot_general(q[:, h * DH_MEM:(h + 1) * DH_MEM], k, (((1,), (1,)), ((), ())),
                            preferred_element_type=F32)
        p = jnp.exp2(s - jnp.max(s, axis=-1, keepdims=True))
        o = jnp.dot(p.astype(BF16), v, preferred_element_type=F32)
        outs.append(o / jnp.sum(p, axis=-1, keepdims=True))
    o = jnp.concatenate(outs, axis=1).astype(BF16)
    y = jnp.dot(o, wo_ref[...], preferred_element_type=F32)
    o_ref[0] = x + _rms(y, post_ref[...])


def _mem_attn(x, kv, pre, wq, wo, post, tm):
    b, l, d = x.shape
    m = kv.shape[1]
    const = lambda shape: pl.BlockSpec(shape, lambda bi, i: (0,) * len(shape))
    return pl.pallas_call(
        _mem_attn_kernel,
        grid=(b, l // tm),
        in_specs=[
            pl.BlockSpec((1, tm, d), lambda bi, i: (bi, i, 0)),
            pl.BlockSpec((1, m, 2 * H_MEM * DH_MEM), lambda bi, i: (bi, 0, 0)),
            const((1, d)), const((d, H_MEM * DH_MEM)), const((H_MEM * DH_MEM, d)), const((1, d)),
        ],
        out_specs=pl.BlockSpec((1, tm, d), lambda bi, i: (bi, i, 0)),
        out_shape=jax.ShapeDtypeStruct((b, l, d), F32),
        compiler_params=_params(("parallel", "parallel")),
        name="mem_attn",
    )(x, kv, pre, wq, wo, post)


def _ffn_kernel(p_ref, x_ref, n_ref, pre_ref, wa_ref, wg_ref, cwa_ref, cwg_ref, cba_ref, cbg_ref, wo_ref, post_ref,
                o_ref, xe_ref, acc_ref):
    tm = x_ref.shape[1]
    c = pl.program_id(2)

    @pl.when(c == 0)
    def _():
        xe = _halo_rows(p_ref, x_ref, n_ref, pl.program_id(1), pl.num_programs(1))
        xe_ref[...] = _rms(xe, pre_ref[...]).astype(BF16)
        acc_ref[...] = jnp.zeros(acc_ref.shape, F32)

    xe = xe_ref[...]
    a = _conv3_rows(jnp.dot(xe, wa_ref[...], preferred_element_type=F32), cwa_ref, cba_ref, tm)
    g = _conv3_rows(jnp.dot(xe, wg_ref[...], preferred_element_type=F32), cwg_ref, cbg_ref, tm)
    act = 0.5 * a * (1.0 + lax.erf(a * (2.0 ** -0.5))) * g
    acc_ref[...] += jnp.dot(act.astype(BF16), wo_ref[...], preferred_element_type=F32)

    @pl.when(c == pl.num_programs(2) - 1)
    def _():
        o_ref[0] = x_ref[0] + _rms(acc_ref[...], post_ref[...])


def _ffn(x, pre, w_in, conv_w, conv_b, w_out, post, tm, tf):
    b, l, d = x.shape
    nf = D_FF // tf
    r8 = tm // 8
    n8 = l // 8
    return pl.pallas_call(
        _ffn_kernel,
        grid=(b, l // tm, nf),
        in_specs=[
            pl.BlockSpec((1, 8, d), lambda bi, i, c: (bi, jnp.maximum(i * r8 - 1, 0), 0)),
            pl.BlockSpec((1, tm, d), lambda bi, i, c: (bi, i, 0)),
            pl.BlockSpec((1, 8, d), lambda bi, i, c: (bi, jnp.minimum((i + 1) * r8, n8 - 1), 0)),
            pl.BlockSpec((1, d), lambda bi, i, c: (0, 0)),
            pl.BlockSpec((d, tf), lambda bi, i, c: (0, c)),
            pl.BlockSpec((d, tf), lambda bi, i, c: (0, nf + c)),
            pl.BlockSpec((3, tf), lambda bi, i, c: (0, c)),
            pl.BlockSpec((3, tf), lambda bi, i, c: (0, nf + c)),
            pl.BlockSpec((1, tf), lambda bi, i, c: (0, c)),
            pl.BlockSpec((1, tf), lambda bi, i, c: (0, nf + c)),
            pl.BlockSpec((tf, d), lambda bi, i, c: (c, 0)),
            pl.BlockSpec((1, d), lambda bi, i, c: (0, 0)),
        ],
        out_specs=pl.BlockSpec((1, tm, d), lambda bi, i, c: (bi, i, 0)),
        out_shape=jax.ShapeDtypeStruct((b, l, d), F32),
        scratch_shapes=[pltpu.VMEM((tm + 16, d), BF16), pltpu.VMEM((tm, d), F32)],
        compiler_params=_params(("parallel", "parallel", "arbitrary")),
        name="conv_ffn",
    )(x, x, x, pre, w_in, w_in, conv_w, conv_w, conv_b, conv_b, w_out, post)


def _rot_half_cols(w, half):
    shp = w.shape
    w = w.reshape(shp[:-1] + (shp[-1] // (2 * half), 2, half))
    return jnp.concatenate([-w[..., 1:2, :], w[..., 0:1, :]], axis=-2).reshape(shp)


def _swap_half(g, half):
    shp = g.shape
    g = g.reshape(shp[:-1] + (shp[-1] // (2 * half), 2, half))
    return jnp.concatenate([g[..., 1:2, :], g[..., 0:1, :]], axis=-2).reshape(shp)


def _pack_w_in(w_in):
    sizes = (512, 512, 512, Q_LORA, KV_LORA, ROPE_B, D_INNER, D_INNER + 2 * N_GROUPS * D_STATE, H_C, H_C,
             H_D * DH_D, KV_D * DH_D, KV_D * DH_D, N_BRANCH * D_MODEL)
    offs = np.concatenate([[0], np.cumsum(sizes)])
    (a_q, a_k, a_v, b_cq, b_ckv, b_kr, c_z, c_xbc, c_dtf, c_dtb, d_q, d_k, d_v, g) = [
        w_in[:, offs[i]:offs[i + 1]] for i in range(len(sizes))]
    d = w_in.shape[0]
    z = lambda n: jnp.zeros((d, n), w_in.dtype)
    kr_blk = jnp.concatenate([z(NOPE_B), b_kr, z(LANE - NOPE_B - ROPE_B)], axis=1)
    krr_blk = jnp.concatenate([z(NOPE_B), _rot_half_cols(b_kr, ROPE_B // 2), z(LANE - NOPE_B - ROPE_B)], axis=1)
    dt_blk = jnp.concatenate([c_dtf, c_dtb, z(LANE - 2 * H_C)], axis=1)
    cols = [
        g,
        a_q * (DH_A ** -0.5 * LOG2E), a_k, a_v,
        c_z, c_xbc[:, :D_INNER],
        d_q, _rot_half_cols(d_q, DH_D // 4),
        b_cq, kr_blk, b_ckv, c_xbc[:, D_INNER:],
        krr_blk, dt_blk,
        d_k, _rot_half_cols(d_k, DH_D // 4), d_v,
        z((N_COLBLK - COL_DV - 1) * LANE),
    ]
    return jnp.concatenate(cols, axis=1).astype(BF16)


def _pad_heads(w, n_heads, per_head, take_from, take, place_at):
    d = w.shape[0]
    w = w.reshape(d, n_heads, per_head)[:, :, take_from:take_from + take]
    w = jnp.pad(w, ((0, 0), (0, 0), (place_at, LANE - place_at - take)))
    return w.reshape(d, n_heads * LANE)


def _rope_tables(l):
    half = ROPE_B // 2
    freqs = ROPE_THETA ** (-jnp.arange(half, dtype=F32) / half)
    pos = jnp.arange(l, dtype=F32)

    def cs(p):
        ang = p[:, None] * freqs[None, :]
        return jnp.cos(ang), jnp.sin(ang)

    c, s = cs(pos)
    ones = jnp.ones((l, NOPE_B), F32)
    zeros = jnp.zeros((l, LANE - NOPE_B - ROPE_B), F32)
    cos_b = jnp.concatenate([ones, c, c, zeros], axis=1)
    sin_b = jnp.concatenate([0 * ones, s, s, zeros], axis=1)
    cr, sr = cs(jnp.floor(pos / GRID_W))
    cc, sc = cs(pos - GRID_W * jnp.floor(pos / GRID_W))
    cos_d = jnp.concatenate([cr, cr, cc, cc] * 2, axis=1)
    sin_d = jnp.concatenate([sr, sr, sc, sc] * 2, axis=1)
    return cos_b, sin_b, cos_d, sin_d


def _layer(x, kv_mem, lp, layer_idx, tables):
    b, l, d = x.shape
    cos_b, sin_b, cos_d, sin_d = tables
    tm = min(512, l)
    t = b * l

    u = _norm_matmul(x.reshape(t, d), lp["norm_pre"][0], lp["w_in_packed"], F32, tm, PROJ_TN)
    u = u.reshape(b, l, N_COLBLK * LANE)

    tq = min(256, l)
    tk = min(512, l)
    qt_a, k_a, vt_a = _prep_a(u, b, l, tm)
    o_a = _flash(qt_a, k_a, vt_a, lp["alibi"], tq=tq, tk=tk, alibi=True)
    qt_b, k_b, vt_b = _prep_b(u, b, l, tm, cos_b, sin_b, lp["mla_q_norm"].reshape(1, -1),
                              lp["mla_kv_norm"].reshape(1, -1), lp["wq_b"], lp["wqr_b"], lp["wk_b"], lp["wv_b"])
    o_b = _flash(qt_b, k_b, vt_b, lp["alibi"], tq=tq, tk=tk, alibi=False)
    qt_d, k_d, vt_d = _prep_d(u, b, l, tm, cos_d, sin_d, lp["gq_d"], lp["gqr_d"], lp["gk_d"], lp["gkr_d"])
    o_d = _flash(qt_d, k_d, vt_d, lp["alibi"], tq=tq, tk=tk, alibi=False)

    xs = _conv_silu(u, b, l, tm, COL_CXS // 4, D_INNER, lp["ssm_conv_w"][:, :D_INNER], lp["ssm_conv_b"][None, :D_INNER])
    bc = _conv_silu(u, b, l, tm, COL_CBC // 2, 2 * LANE, lp["ssm_conv_w"][:, D_INNER:], lp["ssm_conv_b"][None, D_INNER:])
    y_f = _ssd_scan(xs, bc, u, b, l, lp["alog_vec"], lp["dtb_vec"], reverse=False)
    y_c = _ssd_scan(xs, bc, u, b, l, lp["alog_vec"], lp["dtb_vec"], reverse=True,
                    final_args=(y_f, lp["dskip_x"], lp["ssm_norm"].reshape(1, -1)))

    lam_init = 0.8 - 0.6 * math.exp(-0.3 * layer_idx)
    x = _merge(x, o_a, o_b, y_c, o_d, u, lp["diff_lambda"], lp["diff_norm"].reshape(1, -1), lp["w_branch16"],
               lp["w_out16"], lp["norm_post"][0].reshape(1, -1), lam_init, min(256, l))

    x = _mem_attn(x, kv_mem, lp["norm_pre"][1].reshape(1, -1), lp["w_mem_q16"], lp["w_mem_o16"],
                  lp["norm_post"][1].reshape(1, -1), tm)

    x = _ffn(x, lp["norm_pre"][2].reshape(1, -1), lp["w_ffn_in16"], lp["ffn_conv_w"], lp["ffn_conv_b"].reshape(1, -1),
             lp["w_ffn_out16"], lp["norm_post"][2].reshape(1, -1), min(1024, l), 256)
    return x


def _prep_layer_params(p, layer_idx):
    lp = {name: arr[layer_idx] for name, arr in p.items()}
    lp["w_in_packed"] = _pack_w_in(lp["w_in"])
    qscale_b = (NOPE_B + ROPE_B) ** -0.5 * LOG2E
    wq = lp["w_mla_uq"] * qscale_b
    per_q = NOPE_B + ROPE_B
    lp["wq_b"] = _pad_heads(wq, H_B, per_q, 0, per_q, 0).astype(BF16)
    wq_rope = wq.reshape(Q_LORA, H_B, per_q)[:, :, NOPE_B:]
    lp["wqr_b"] = _pad_heads(_rot_half_cols(wq_rope, ROPE_B // 2).reshape(Q_LORA, H_B * ROPE_B), H_B, ROPE_B, 0, ROPE_B,
                             NOPE_B).astype(BF16)
    per_kv = NOPE_B + V_B
    lp["wk_b"] = _pad_heads(lp["w_mla_ukv"], H_B, per_kv, 0, NOPE_B, 0).astype(BF16)
    lp["wv_b"] = _pad_heads(lp["w_mla_ukv"], H_B, per_kv, NOPE_B, V_B, 0).astype(BF16)
    tile2 = lambda g: jnp.tile(g, 2).reshape(1, LANE)
    lp["gq_d"] = tile2(lp["gqa_q_norm"])
    lp["gqr_d"] = tile2(_swap_half(lp["gqa_q_norm"], DH_D // 4))
    lp["gk_d"] = tile2(lp["gqa_k_norm"])
    lp["gkr_d"] = tile2(_swap_half(lp["gqa_k_norm"], DH_D // 4))
    slopes = np.array([2.0 ** (-8.0 * (i + 1) / H_A) for i in range(H_A)], np.float32) * np.float32(LOG2E)
    lp["alibi"] = jnp.asarray(np.repeat(slopes, 2))
    pad_vec = lambda v: jnp.pad(v.reshape(-1), (0, LANE - 2 * H_C)).reshape(1, LANE)
    lp["alog_vec"] = pad_vec(lp["ssm_a_log"])
    lp["dtb_vec"] = pad_vec(lp["ssm_dt_bias"])
    lp["dskip_x"] = jnp.repeat(lp["ssm_d"], SSM_HEAD).reshape(1, D_INNER)
    lp["w_branch16"] = lp["w_branch"].astype(BF16)
    lp["w_out16"] = lp["w_out"].astype(BF16)
    lp["w_mem_q16"] = (lp["w_mem_q"] * (DH_MEM ** -0.5 * LOG2E)).astype(BF16)
    lp["w_mem_kv16"] = lp["w_mem_kv"].astype(BF16)
    lp["w_mem_o16"] = lp["w_mem_o"].astype(BF16)
    lp["w_ffn_in16"] = lp["w_ffn_in"].astype(BF16)
    lp["w_ffn_out16"] = lp["w_ffn_out"].astype(BF16)
    return lp


def _trunk(x, mem, layer_params):
    b, l, d = x.shape
    m = mem.shape[1]
    tables = _rope_tables(l)
    for layer_idx, lp in enumerate(layer_params):
        kv = _norm_matmul(mem.reshape(b * m, d), lp["mem_norm"], lp["w_mem_kv16"], BF16, min(256, b * m), 512)
        x = _layer(x, kv.reshape(b, m, 2 * H_MEM * DH_MEM), lp, layer_idx, tables)
    return x


def kernel(x_prompt, x_sample, mem_prompt, mem_sample, w_in, w_branch, w_out, diff_lambda, diff_norm, mla_q_norm,
           mla_kv_norm, w_mla_uq, w_mla_ukv, ssm_conv_w, ssm_conv_b, ssm_a_log, ssm_dt_bias, ssm_d, ssm_norm,
           gqa_q_norm, gqa_k_norm, mem_norm, w_mem_q, w_mem_kv, w_mem_o, w_ffn_in, ffn_conv_w, ffn_conv_b, w_ffn_out,
           norm_pre, norm_post):
    p = {
        "w_in": w_in, "w_branch": w_branch, "w_out": w_out, "diff_lambda": diff_lambda, "diff_norm": diff_norm,
        "mla_q_norm": mla_q_norm, "mla_kv_norm": mla_kv_norm, "w_mla_uq": w_mla_uq, "w_mla_ukv": w_mla_ukv,
        "ssm_conv_w": ssm_conv_w, "ssm_conv_b": ssm_conv_b, "ssm_a_log": ssm_a_log, "ssm_dt_bias": ssm_dt_bias,
        "ssm_d": ssm_d, "ssm_norm": ssm_norm, "gqa_q_norm": gqa_q_norm, "gqa_k_norm": gqa_k_norm,
        "mem_norm": mem_norm, "w_mem_q": w_mem_q, "w_mem_kv": w_mem_kv, "w_mem_o": w_mem_o,
        "w_ffn_in": w_ffn_in, "ffn_conv_w": ffn_conv_w, "ffn_conv_b": ffn_conv_b, "w_ffn_out": w_ffn_out,
        "norm_pre": norm_pre, "norm_post": norm_post,
    }
    layer_params = [_prep_layer_params(p, i) for i in range(DEPTH)]
    y_prompt = _trunk(x_prompt, mem_prompt, layer_params)
    y_sample = _trunk(x_sample, mem_sample, layer_params)
    return (y_prompt, y_sample)
```

```python
import functools
import math

import jax
import jax.numpy as jnp
import numpy as np
from jax import lax
from jax.experimental import pallas as pl
from jax.experimental.pallas import tpu as pltpu

F32 = jnp.float32
BF16 = jnp.bfloat16
HIGHEST = lax.Precision.HIGHEST

D_MODEL = 1024
DEPTH = 2
GRID_W = 64
ROPE_THETA = 10000.0
EPS = 1e-6
N_BRANCH = 4
BRANCH_W = 512
H_A = 4
DH_A = 64
H_B = 4
Q_LORA = 384
KV_LORA = 256
NOPE_B = 64
ROPE_B = 32
V_B = 128
D_INNER = 512
SSM_HEAD = 64
H_C = D_INNER // SSM_HEAD
N_GROUPS = 2
D_STATE = 64
CHUNK = 128
H_D = 8
KV_D = 2
DH_D = 64
H_MEM = 4
DH_MEM = 128
D_FF = 2816

LANE = 128
LOG2E = 1.4426950408889634
NEG_BIG = -1e30
ONES_ROWS = 16
VMEM_LIMIT = 56 * 1024 * 1024

COL_G = 0
COL_AQ = 32
COL_AK = 36
COL_AV = 40
COL_CZ = 44
COL_CXS = 48
COL_DQ = 52
COL_DQR = 56
COL_BCQ = 60
COL_BKR = 63
COL_BCKV = 64
COL_CBC = 66
COL_BKRR = 68
COL_CDT = 69
COL_DK = 70
COL_DKR = 71
COL_DV = 72
N_COLBLK = 76
PROJ_TN = 512


def _params(sem):
    return pltpu.CompilerParams(dimension_semantics=sem, vmem_limit_bytes=VMEM_LIMIT)


def _rms(x, g):
    return x * lax.rsqrt(jnp.mean(x * x, axis=-1, keepdims=True) + EPS) * g


def _norm_matmul_kernel(x_ref, g_ref, w_ref, o_ref, xn_ref):
    @pl.when(pl.program_id(1) == 0)
    def _():
        xn_ref[...] = _rms(x_ref[...], g_ref[...]).astype(BF16)

    o_ref[...] = jnp.dot(xn_ref[...], w_ref[...], preferred_element_type=F32).astype(o_ref.dtype)


def _norm_matmul(x, g, w, out_dtype, tm, tn):
    t, d = x.shape
    n = w.shape[1]
    return pl.pallas_call(
        _norm_matmul_kernel,
        grid=(t // tm, n // tn),
        in_specs=[
            pl.BlockSpec((tm, d), lambda i, j: (i, 0)),
            pl.BlockSpec((1, d), lambda i, j: (0, 0)),
            pl.BlockSpec((d, tn), lambda i, j: (0, j)),
        ],
        out_specs=pl.BlockSpec((tm, tn), lambda i, j: (i, j)),
        out_shape=jax.ShapeDtypeStruct((t, n), out_dtype),
        scratch_shapes=[pltpu.VMEM((tm, d), BF16)],
        compiler_params=_params(("parallel", "arbitrary")),
        name="norm_matmul",
    )(x, g.reshape(1, d), w)


def _ones_rows(tm):
    row = lax.broadcasted_iota(jnp.int32, (ONES_ROWS, tm), 0)
    return jnp.where(row == 0, 1.0, 0.0).astype(BF16)


def _prep_a_kernel(q_ref, k_ref, v_ref, qt_ref, ko_ref, vt_ref):
    q = q_ref[0]
    tm = q.shape[0]
    lane = lax.broadcasted_iota(jnp.int32, q.shape, 1)
    q1 = jnp.where(lane < DH_A, q, 0.0)
    qt_ref[0, 0] = q1.T.astype(BF16)
    qt_ref[0, 1] = (q - q1).T.astype(BF16)
    ko_ref[0, 0] = k_ref[0].astype(BF16)
    vt_ref[0, 0, 0:2 * DH_A, :] = v_ref[0].T.astype(BF16)
    vt_ref[0, 0, 2 * DH_A:, :] = _ones_rows(tm)


def _prep_a(u, b, l, tm):
    dv = 2 * DH_A + ONES_ROWS
    return pl.pallas_call(
        _prep_a_kernel,
        grid=(b, l // tm, H_A),
        in_specs=[
            pl.BlockSpec((1, tm, LANE), lambda bi, i, h: (bi, i, COL_AQ + h)),
            pl.BlockSpec((1, tm, LANE), lambda bi, i, h: (bi, i, COL_AK + h)),
            pl.BlockSpec((1, tm, LANE), lambda bi, i, h: (bi, i, COL_AV + h)),
        ],
        out_specs=[
            pl.BlockSpec((1, 2, LANE, tm), lambda bi, i, h: (bi, h, 0, i)),
            pl.BlockSpec((1, 1, tm, LANE), lambda bi, i, h: (bi, h, i, 0)),
            pl.BlockSpec((1, 1, dv, tm), lambda bi, i, h: (bi, h, 0, i)),
        ],
        out_shape=[
            jax.ShapeDtypeStruct((b, 2 * H_A, LANE, l), BF16),
            jax.ShapeDtypeStruct((b, H_A, l, LANE), BF16),
            jax.ShapeDtypeStruct((b, H_A, dv, l), BF16),
        ],
        compiler_params=_params(("parallel", "parallel", "parallel")),
        name="prep_diff",
    )(u, u, u)


def _prep_b_kernel(cq_ref, ckv_ref, kr_ref, krr_ref, cos_ref, sin_ref, gq_ref, gkv_ref,
                   wq_ref, wqr_ref, wk_ref, wv_ref, qt_ref, ko_ref, vt_ref):
    tm = cq_ref.shape[1]
    cos = cos_ref[...]
    sin = sin_ref[...]
    cqn = _rms(cq_ref[0], gq_ref[...]).astype(BF16)
    ckn = _rms(ckv_ref[0], gkv_ref[...]).astype(BF16)
    qa = jnp.dot(cqn, wq_ref[...], preferred_element_type=F32)
    qb = jnp.dot(cqn, wqr_ref[...], preferred_element_type=F32)
    kn = jnp.dot(ckn, wk_ref[...], preferred_element_type=F32)
    vv = jnp.dot(ckn, wv_ref[...], preferred_element_type=F32)
    krope = kr_ref[0] * cos + krr_ref[0] * sin
    ones = _ones_rows(tm)
    for h in range(H_B):
        sl = slice(h * LANE, (h + 1) * LANE)
        qh = qa[:, sl] * cos + qb[:, sl] * sin
        qt_ref[0, h] = qh.T.astype(BF16)
        ko_ref[0, h] = (kn[:, sl] + krope).astype(BF16)
        vt_ref[0, h, 0:V_B, :] = vv[:, sl].T.astype(BF16)
        vt_ref[0, h, V_B:, :] = ones


def _prep_b(u, b, l, tm, cos, sin, gq, gkv, wq, wqr, wk, wv):
    dv = V_B + ONES_ROWS
    const = lambda shape: pl.BlockSpec(shape, lambda bi, i: (0,) * len(shape))
    return pl.pallas_call(
        _prep_b_kernel,
        grid=(b, l // tm),
        in_specs=[
            pl.BlockSpec((1, tm, Q_LORA), lambda bi, i: (bi, i, COL_BCQ // 3)),
            pl.BlockSpec((1, tm, KV_LORA), lambda bi, i: (bi, i, COL_BCKV // 2)),
            pl.BlockSpec((1, tm, LANE), lambda bi, i: (bi, i, COL_BKR)),
            pl.BlockSpec((1, tm, LANE), lambda bi, i: (bi, i, COL_BKRR)),
            pl.BlockSpec((tm, LANE), lambda bi, i: (i, 0)),
            pl.BlockSpec((tm, LANE), lambda bi, i: (i, 0)),
            const((1, Q_LORA)), const((1, KV_LORA)),
            const((Q_LORA, H_B * LANE)), const((Q_LORA, H_B * LANE)),
            const((KV_LORA, H_B * LANE)), const((KV_LORA, H_B * LANE)),
        ],
        out_specs=[
            pl.BlockSpec((1, H_B, LANE, tm), lambda bi, i: (bi, 0, 0, i)),
            pl.BlockSpec((1, H_B, tm, LANE), lambda bi, i: (bi, 0, i, 0)),
            pl.BlockSpec((1, H_B, dv, tm), lambda bi, i: (bi, 0, 0, i)),
        ],
        out_shape=[
            jax.ShapeDtypeStruct((b, H_B, LANE, l), BF16),
            jax.ShapeDtypeStruct((b, H_B, l, LANE), BF16),
            jax.ShapeDtypeStruct((b, H_B, dv, l), BF16),
        ],
        compiler_params=_params(("parallel", "parallel")),
        name="prep_mla",
    )(u, u, u, u, cos, sin, gq, gkv, wq, wqr, wk, wv)


def _prep_d_kernel(q_ref, qr_ref, k_ref, kr_ref, v_ref, cos_ref, sin_ref, gq_ref, gqr_ref, gk_ref, gkr_ref,
                   qt_ref, ko_ref, vt_ref, *, qscale):
    tm = q_ref.shape[1]
    cos = cos_ref[...]
    sin = sin_ref[...]
    r_i = lax.broadcasted_iota(jnp.int32, (LANE, LANE), 0) // DH_D
    c_i = lax.broadcasted_iota(jnp.int32, (LANE, LANE), 1) // DH_D
    blockdiag = jnp.where(r_i == c_i, 1.0, 0.0).astype(F32)

    def norm_rope(x, xr, g, gr):
        ss = jnp.dot(x * x, blockdiag, precision=HIGHEST, preferred_element_type=F32)
        r = lax.rsqrt(ss * (1.0 / DH_D) + EPS)
        return r * (x * g * cos + xr * gr * sin)

    q = q_ref[0]
    qr = qr_ref[0]
    zeros = jnp.zeros((DH_D, tm), F32)
    for j in range(H_D // 2):
        sl = slice(j * LANE, (j + 1) * LANE)
        y = norm_rope(q[:, sl], qr[:, sl], gq_ref[...], gqr_ref[...]) * qscale
        yt = y.T
        grp = (2 * j) // (H_D // KV_D)
        for e in range(2):
            piece = yt[e * DH_D:(e + 1) * DH_D]
            full = jnp.concatenate([piece, zeros], axis=0) if grp == 0 else jnp.concatenate([zeros, piece], axis=0)
            qt_ref[0, 2 * j + e] = full.astype(BF16)
    ko_ref[0, 0] = norm_rope(k_ref[0], kr_ref[0], gk_ref[...], gkr_ref[...]).astype(BF16)
    vt = v_ref[0].T
    ones = _ones_rows(tm)
    for grp in range(KV_D):
        vt_ref[0, grp, 0:DH_D, :] = vt[grp * DH_D:(grp + 1) * DH_D].astype(BF16)
        vt_ref[0, grp, DH_D:, :] = ones


def _prep_d(u, b, l, tm, cos, sin, gq, gqr, gk, gkr):
    dv = DH_D + ONES_ROWS
    const = lambda shape: pl.BlockSpec(shape, lambda bi, i: (0,) * len(shape))
    return pl.pallas_call(
        functools.partial(_prep_d_kernel, qscale=DH_D ** -0.5 * LOG2E),
        grid=(b, l // tm),
        in_specs=[
            pl.BlockSpec((1, tm, 4 * LANE), lambda bi, i: (bi, i, COL_DQ // 4)),
            pl.BlockSpec((1, tm, 4 * LANE), lambda bi, i: (bi, i, COL_DQR // 4)),
            pl.BlockSpec((1, tm, LANE), lambda bi, i: (bi, i, COL_DK)),
            pl.BlockSpec((1, tm, LANE), lambda bi, i: (bi, i, COL_DKR)),
            pl.BlockSpec((1, tm, LANE), lambda bi, i: (bi, i, COL_DV)),
            pl.BlockSpec((tm, LANE), lambda bi, i: (i, 0)),
            pl.BlockSpec((tm, LANE), lambda bi, i: (i, 0)),
            const((1, LANE)), const((1, LANE)), const((1, LANE)), const((1, LANE)),
        ],
        out_specs=[
            pl.BlockSpec((1, H_D, LANE, tm), lambda bi, i: (bi, 0, 0, i)),
            pl.BlockSpec((1, 1, tm, LANE), lambda bi, i: (bi, 0, i, 0)),
            pl.BlockSpec((1, KV_D, dv, tm), lambda bi, i: (bi, 0, 0, i)),
        ],
        out_shape=[
            jax.ShapeDtypeStruct((b, H_D, LANE, l), BF16),
            jax.ShapeDtypeStruct((b, 1, l, LANE), BF16),
            jax.ShapeDtypeStruct((b, KV_D, dv, l), BF16),
        ],
        compiler_params=_params(("parallel", "parallel")),
        name="prep_gqa",
    )(u, u, u, u, u, cos, sin, gq, gqr, gk, gkr)


def _flash_kernel(slope_ref, qt_ref, k_ref, vt_ref, o_ref, m_ref, acc_ref, sa_ref, sb_ref, *, tk, dv, alibi):
    tq = qt_ref.shape[3]
    l = k_ref.shape[2]
    n_pairs = l // (2 * tk)
    m_ref[...] = jnp.full(m_ref.shape, NEG_BIG, F32)
    acc_ref[...] = jnp.zeros(acc_ref.shape, F32)
    if alibi:
        slope = slope_ref[pl.program_id(1)]
        q0 = pl.program_id(2) * tq
        rel = (lax.broadcasted_iota(jnp.int32, (tk, tq), 0) - lax.broadcasted_iota(jnp.int32, (tk, tq), 1))

    def scores(k0):
        s = jnp.dot(k_ref[0, 0, pl.ds(k0, tk), :], qt_ref[0, 0], preferred_element_type=F32)
        if alibi:
            s = s - slope * jnp.abs(rel + (k0 - q0)).astype(F32)
        return s

    def accumulate(s, k0):
        m_old = m_ref[...]
        m_new = jnp.maximum(m_old, jnp.max(s, axis=0, keepdims=True))
        alpha = jnp.exp2(m_old - m_new)
        p = jnp.exp2(s - m_new).astype(BF16)
        pv = jnp.dot(vt_ref[0, 0, :, pl.ds(k0, tk)], p, preferred_element_type=F32)
        acc_ref[...] = alpha * acc_ref[...] + pv
        m_ref[...] = m_new

    sa_ref[...] = scores(0)

    def body(j, carry):
        k0 = pl.multiple_of(j * (2 * tk), 2 * tk)
        k1 = pl.multiple_of(k0 + tk, tk)
        sb_ref[...] = scores(k1)
        accumulate(sa_ref[...], k0)
        k2 = pl.multiple_of(jnp.minimum(k0 + 2 * tk, l - tk), tk)
        sa_ref[...] = scores(k2)
        accumulate(sb_ref[...], k1)
        return carry

    lax.fori_loop(0, n_pairs, body, 0)
    acc = acc_ref[...]
    o_ref[0, 0] = acc[0:dv] / acc[dv:dv + 1]


def _flash(qt, k, vt, slopes, *, tq, tk, alibi):
    b, hq, _, l = qt.shape
    hk = k.shape[1]
    hv = vt.shape[1]
    dvp = vt.shape[2]
    dv = dvp - ONES_ROWS
    return pl.pallas_call(
        functools.partial(_flash_kernel, tk=tk, dv=dv, alibi=alibi),
        grid=(b, hq, l // tq),
        in_specs=[
            pl.BlockSpec(memory_space=pltpu.SMEM),
            pl.BlockSpec((1, 1, LANE, tq), lambda bi, h, i: (bi, h, 0, i)),
            pl.BlockSpec((1, 1, l, LANE), lambda bi, h, i: (bi, h // (hq // hk), 0, 0)),
            pl.BlockSpec((1, 1, dvp, l), lambda bi, h, i: (bi, h // (hq // hv), 0, 0)),
        ],
        out_specs=pl.BlockSpec((1, 1, dv, tq), lambda bi, h, i: (bi, h, 0, i)),
        out_shape=jax.ShapeDtypeStruct((b, hq, dv, l), F32),
        scratch_shapes=[pltpu.VMEM((1, tq), F32), pltpu.VMEM((dvp, tq), F32),
                        pltpu.VMEM((tk, tq), F32), pltpu.VMEM((tk, tq), F32)],
        compiler_params=_params(("parallel", "parallel", "parallel")),
        name="flash_" + ("alibi" if alibi else "plain") + f"_{hq}_{dv}",
    )(slopes, qt, k, vt)


def _conv3_rows(xe, w_ref, b_ref, tm):
    n = xe.shape[0]
    up = pltpu.roll(xe, 1, 0)
    dn = pltpu.roll(xe, n - 1, 0)
    y = up * w_ref[0:1, :] + xe * w_ref[1:2, :] + dn * w_ref[2:3, :] + b_ref[...]
    return y[8:8 + tm]


def _halo_rows(prev_ref, main_ref, next_ref, i, n_i):
    prev = jnp.where(i == 0, 0.0, prev_ref[0])
    nxt = jnp.where(i == n_i - 1, 0.0, next_ref[0])
    return jnp.concatenate([prev, main_ref[0], nxt], axis=0)


def _conv_silu_kernel(p_ref, x_ref, n_ref, w_ref, b_ref, o_ref):
    tm = x_ref.shape[1]
    xe = _halo_rows(p_ref, x_ref, n_ref, pl.program_id(1), pl.num_programs(1))
    y = _conv3_rows(xe, w_ref, b_ref, tm)
    o_ref[0] = y * (1.0 / (1.0 + jnp.exp(-y)))


def _conv_silu(u, b, l, tm, col_blk, width, w, bias):
    r8 = tm // 8
    n8 = l // 8
    return pl.pallas_call(
        _conv_silu_kernel,
        grid=(b, l // tm),
        in_specs=[
            pl.BlockSpec((1, 8, width), lambda bi, i: (bi, jnp.maximum(i * r8 - 1, 0), col_blk)),
            pl.BlockSpec((1, tm, width), lambda bi, i: (bi, i, col_blk)),
            pl.BlockSpec((1, 8, width), lambda bi, i: (bi, jnp.minimum((i + 1) * r8, n8 - 1), col_blk)),
            pl.BlockSpec((3, width), lambda bi, i: (0, 0)),
            pl.BlockSpec((1, width), lambda bi, i: (0, 0)),
        ],
        out_specs=pl.BlockSpec((1, tm, width), lambda bi, i: (bi, i, 0)),
        out_shape=jax.ShapeDtypeStruct((b, l, width), F32),
        compiler_params=_params(("parallel", "parallel")),
        name=f"conv_silu_{width}",
    )(u, u, u, w, bias)


def _ssd_kernel(*refs, reverse, final):
    if final:
        (xs_ref, bc_b_ref, bc_c_ref, dt_ref, alog_ref, dtb_ref, yprev_ref, z_ref, dskip_ref, norm_ref,
         y_ref, state_ref) = refs
    else:
        xs_ref, bc_b_ref, bc_c_ref, dt_ref, alog_ref, dtb_ref, y_ref, state_ref = refs
    q = CHUNK

    @pl.when(pl.program_id(1) == 0)
    def _():
        state_ref[...] = jnp.zeros(state_ref.shape, F32)

    row = lax.broadcasted_iota(jnp.int32, (q, q), 0)
    col = lax.broadcasted_iota(jnp.int32, (q, q), 1)
    mask = (col >= row) if reverse else (col <= row)
    tri = jnp.where(mask, 1.0, 0.0).astype(F32)
    head_off = H_C if reverse else 0

    x = dt_ref[0] + dtb_ref[...]
    dt = jnp.maximum(x, 0.0) + jnp.log(1.0 + jnp.exp(-jnp.abs(x)))
    a = dt * (-jnp.exp(alog_ref[...]))
    cs = jnp.dot(tri, a, precision=HIGHEST, preferred_element_type=F32)
    e_r = lax.broadcasted_iota(jnp.int32, (q, D_INNER), 0)
    e_c = lax.broadcasted_iota(jnp.int32, (q, D_INNER), 1) // SSM_HEAD + head_off
    expand = jnp.where(e_r == e_c, 1.0, 0.0).astype(F32)
    dt_x = jnp.dot(dt, expand, precision=HIGHEST, preferred_element_type=F32)
    cs_x = jnp.dot(cs, expand, precision=HIGHEST, preferred_element_type=F32)
    tot_x = cs_x[0:1] if reverse else cs_x[q - 1:q]

    xs = xs_ref[0]
    xdt = xs * dt_x
    w_in = (xdt * jnp.exp(tot_x - cs_x)).astype(BF16)
    bt = bc_b_ref[0].T
    bt16 = bt.astype(BF16)
    cm = bc_c_ref[0]
    cs_t = cs.T
    lane = lax.broadcasted_iota(jnp.int32, (q, LANE), 1)
    rowi = lax.broadcasted_iota(jnp.int32, (q, LANE), 0)

    outs = []
    for pair in range(H_C // 2):
        grp = (2 * pair) // (H_C // N_GROUPS)
        in_grp = (lane // D_STATE) == grp
        c_g = jnp.where(in_grp, cm, 0.0).astype(BF16)
        cb = jnp.dot(c_g, bt16, preferred_element_type=F32)
        psl = slice(pair * LANE, (pair + 1) * LANE)
        x_p = xdt[:, psl].astype(BF16)
        ys = []
        for e in range(2):
            hl = head_off + 2 * pair + e
            seg = cs[:, hl:hl + 1] - cs_t[hl:hl + 1, :]
            dec = jnp.exp(jnp.where(mask, seg, -jnp.inf))
            ys.append(jnp.dot((cb * dec).astype(BF16), x_p, preferred_element_type=F32))
        y_diag = jnp.where(lane < SSM_HEAD, ys[0], ys[1])
        st = state_ref[pair]
        y_off = jnp.dot(c_g, st.astype(BF16), preferred_element_type=F32) * jnp.exp(cs_x[:, psl])
        outs.append(y_diag + y_off)
        upd = jnp.dot(bt16, w_in[:, psl], preferred_element_type=F32)
        keep = (rowi // D_STATE) == grp
        state_ref[pair] = jnp.where(keep, st * jnp.exp(tot_x[:, psl]) + upd, 0.0)
    y = jnp.concatenate(outs, axis=1)
    if final:
        y = y + yprev_ref[0] + dskip_ref[...] * xs
        z = z_ref[0]
        y = y * (z * (1.0 / (1.0 + jnp.exp(-z))))
        y = _rms(y, norm_ref[...])
    y_ref[0] = y


def _ssd_scan(xs, bc, u, b, l, alog, dtb, reverse, final_args=None):
    nc = l // CHUNK
    cidx = (lambda c: nc - 1 - c) if reverse else (lambda c: c)
    in_specs = [
        pl.BlockSpec((1, CHUNK, D_INNER), lambda bi, c: (bi, cidx(c), 0)),
        pl.BlockSpec((1, CHUNK, LANE), lambda bi, c: (bi, cidx(c), 0)),
        pl.BlockSpec((1, CHUNK, LANE), lambda bi, c: (bi, cidx(c), 1)),
        pl.BlockSpec((1, CHUNK, LANE), lambda bi, c: (bi, cidx(c), COL_CDT)),
        pl.BlockSpec((1, LANE), lambda bi, c: (0, 0)),
        pl.BlockSpec((1, LANE), lambda bi, c: (0, 0)),
    ]
    args = [xs, bc, bc, u, alog, dtb]
    final = final_args is not None
    if final:
        yprev, dskip, norm = final_args
        in_specs += [
            pl.BlockSpec((1, CHUNK, D_INNER), lambda bi, c: (bi, cidx(c), 0)),
            pl.BlockSpec((1, CHUNK, D_INNER), lambda bi, c: (bi, cidx(c), COL_CZ // 4)),
            pl.BlockSpec((1, D_INNER), lambda bi, c: (0, 0)),
            pl.BlockSpec((1, D_INNER), lambda bi, c: (0, 0)),
        ]
        args += [yprev, u, dskip, norm]
    return pl.pallas_call(
        functools.partial(_ssd_kernel, reverse=reverse, final=final),
        grid=(b, nc),
        in_specs=in_specs,
        out_specs=pl.BlockSpec((1, CHUNK, D_INNER), lambda bi, c: (bi, cidx(c), 0)),
        out_shape=jax.ShapeDtypeStruct((b, l, D_INNER), F32),
        scratch_shapes=[pltpu.VMEM((H_C // 2, LANE, LANE), F32)],
        compiler_params=_params(("parallel", "arbitrary")),
        name="ssd_bwd" if reverse else "ssd_fwd",
    )(*args)


def _merge_kernel(x_ref, oa_ref, ob_ref, yc_ref, od_ref, g_ref, lam_ref, subln_ref, wb_ref, wo_ref, post_ref,
                  o_ref, *, lam_init):
    lp = lam_ref[...]
    lam = (jnp.exp(jnp.sum(lp[0:1] * lp[1:2], axis=-1, keepdims=True))
           - jnp.exp(jnp.sum(lp[2:3] * lp[3:4], axis=-1, keepdims=True)) + lam_init)
    ya = []
    for h in range(H_A):
        o = (oa_ref[0, 2 * h] - lam * oa_ref[0, 2 * h + 1]).T
        ya.append(_rms(o, subln_ref[...]) * (1.0 - lam_init))
    ya = jnp.concatenate(ya, axis=1).astype(BF16)
    yb = jnp.concatenate([ob_ref[0, h].T for h in range(H_B)], axis=1).astype(BF16)
    yc = yc_ref[0].astype(BF16)
    yd = jnp.concatenate([jnp.concatenate([od_ref[0, 2 * j], od_ref[0, 2 * j + 1]], axis=0).T
                          for j in range(H_D // 2)], axis=1).astype(BF16)
    merged = None
    for n, y in enumerate((ya, yb, yc, yd)):
        proj = jnp.dot(y, wb_ref[n], preferred_element_type=F32)
        gn = g_ref[0, :, n * D_MODEL:(n + 1) * D_MODEL]
        term = proj * (1.0 / (1.0 + jnp.exp(-gn)))
        merged = term if merged is None else merged + term
    out = jnp.dot(merged.astype(BF16), wo_ref[...], preferred_element_type=F32)
    o_ref[0] = x_ref[0] + _rms(out, post_ref[...])


def _merge(x, oa, ob, yc, od, u, lam_p, subln, wb, wo, post, lam_init, tm):
    b, l, d = x.shape
    const = lambda shape: pl.BlockSpec(shape, lambda bi, i: (0,) * len(shape))
    return pl.pallas_call(
        functools.partial(_merge_kernel, lam_init=lam_init),
        grid=(b, l // tm),
        in_specs=[
            pl.BlockSpec((1, tm, d), lambda bi, i: (bi, i, 0)),
            pl.BlockSpec((1, 2 * H_A, LANE, tm), lambda bi, i: (bi, 0, 0, i)),
            pl.BlockSpec((1, H_B, V_B, tm), lambda bi, i: (bi, 0, 0, i)),
            pl.BlockSpec((1, tm, D_INNER), lambda bi, i: (bi, i, 0)),
            pl.BlockSpec((1, H_D, DH_D, tm), lambda bi, i: (bi, 0, 0, i)),
            pl.BlockSpec((1, tm, N_BRANCH * d), lambda bi, i: (bi, i, COL_G // 32)),
            const((4, DH_A)), const((1, 2 * DH_A)),
            const((N_BRANCH, BRANCH_W, d)), const((d, d)), const((1, d)),
        ],
        out_specs=pl.BlockSpec((1, tm, d), lambda bi, i: (bi, i, 0)),
        out_shape=jax.ShapeDtypeStruct((b, l, d), F32),
        compiler_params=_params(("parallel", "parallel")),
        name="merge",
    )(x, oa, ob, yc, od, u, lam_p, subln, wb, wo, post)


def _mem_attn_kernel(x_ref, kv_ref, pre_ref, wq_ref, wo_ref, post_ref, o_ref):
    x = x_ref[0]
    xn = _rms(x, pre_ref[...]).astype(BF16)
    q = jnp.dot(xn, wq_ref[...], preferred_element_type=F32).astype(BF16)
    outs = []
    for h in range(H_MEM):
        k = kv_ref[0, :, 2 * h * DH_MEM:(2 * h + 1) * DH_MEM]
        v = kv_ref[0, :, (2 * h + 1) * DH_MEM:(2 * h + 2) * DH_MEM]
        s = lax.dot_general(q[:, h * DH_MEM:(h + 1) * DH_MEM], k, (((1,), (1,)), ((), ())),
                            preferred_element_type=F32)
        p = jnp.exp2(s - jnp.max(s, axis=-1, keepdims=True))
        o = jnp.dot(p.astype(BF16), v, preferred_element_type=F32)
        outs.append(o / jnp.sum(p, axis=-1, keepdims=True))
    o = jnp.concatenate(outs, axis=1).astype(BF16)
    y = jnp.dot(o, wo_ref[...], preferred_element_type=F32)
    o_ref[0] = x + _rms(y, post_ref[...])


def _mem_attn(x, kv, pre, wq, wo, post, tm):
    b, l, d = x.shape
    m = kv.shape[1]
    const = lambda shape: pl.BlockSpec(shape, lambda bi, i: (0,) * len(shape))
    return pl.pallas_call(
        _mem_attn_kernel,
        grid=(b, l // tm),
        in_specs=[
            pl.BlockSpec((1, tm, d), lambda bi, i: (bi, i, 0)),
            pl.BlockSpec((1, m, 2 * H_MEM * DH_MEM), lambda bi, i: (bi, 0, 0)),
            const((1, d)), const((d, H_MEM * DH_MEM)), const((H_MEM * DH_MEM, d)), const((1, d)),
        ],
        out_specs=pl.BlockSpec((1, tm, d), lambda bi, i: (bi, i, 0)),
        out_shape=jax.ShapeDtypeStruct((b, l, d), F32),
        compiler_params=_params(("parallel", "parallel")),
        name="mem_attn",
    )(x, kv, pre, wq, wo, post)


def _ffn_kernel(p_ref, x_ref, n_ref, pre_ref, wa_ref, wg_ref, cwa_ref, cwg_ref, cba_ref, cbg_ref, wo_ref, post_ref,
                o_ref, xe_ref, acc_ref):
    tm = x_ref.shape[1]
    c = pl.program_id(2)

    @pl.when(c == 0)
    def _():
        xe = _halo_rows(p_ref, x_ref, n_ref, pl.program_id(1), pl.num_programs(1))
        xe_ref[...] = _rms(xe, pre_ref[...]).astype(BF16)
        acc_ref[...] = jnp.zeros(acc_ref.shape, F32)

    xe = xe_ref[...]
    a = _conv3_rows(jnp.dot(xe, wa_ref[...], preferred_element_type=F32), cwa_ref, cba_ref, tm)
    g = _conv3_rows(jnp.dot(xe, wg_ref[...], preferred_element_type=F32), cwg_ref, cbg_ref, tm)
    act = 0.5 * a * (1.0 + lax.erf(a * (2.0 ** -0.5))) * g
    acc_ref[...] += jnp.dot(act.astype(BF16), wo_ref[...], preferred_element_type=F32)

    @pl.when(c == pl.num_programs(2) - 1)
    def _():
        o_ref[0] = x_ref[0] + _rms(acc_ref[...], post_ref[...])


def _ffn(x, pre, w_in, conv_w, conv_b, w_out, post, tm, tf):
    b, l, d = x.shape
    nf = D_FF // tf
    r8 = tm // 8
    n8 = l // 8
    return pl.pallas_call(
        _ffn_kernel,
        grid=(b, l // tm, nf),
        in_specs=[
            pl.BlockSpec((1, 8, d), lambda bi, i, c: (bi, jnp.maximum(i * r8 - 1, 0), 0)),
            pl.BlockSpec((1, tm, d), lambda bi, i, c: (bi, i, 0)),
            pl.BlockSpec((1, 8, d), lambda bi, i, c: (bi, jnp.minimum((i + 1) * r8, n8 - 1), 0)),
            pl.BlockSpec((1, d), lambda bi, i, c: (0, 0)),
            pl.BlockSpec((d, tf), lambda bi, i, c: (0, c)),
            pl.BlockSpec((d, tf), lambda bi, i, c: (0, nf + c)),
            pl.BlockSpec((3, tf), lambda bi, i, c: (0, c)),
            pl.BlockSpec((3, tf), lambda bi, i, c: (0, nf + c)),
            pl.BlockSpec((1, tf), lambda bi, i, c: (0, c)),
            pl.BlockSpec((1, tf), lambda bi, i, c: (0, nf + c)),
            pl.BlockSpec((tf, d), lambda bi, i, c: (c, 0)),
            pl.BlockSpec((1, d), lambda bi, i, c: (0, 0)),
        ],
        out_specs=pl.BlockSpec((1, tm, d), lambda bi, i, c: (bi, i, 0)),
        out_shape=jax.ShapeDtypeStruct((b, l, d), F32),
        scratch_shapes=[pltpu.VMEM((tm + 16, d), BF16), pltpu.VMEM((tm, d), F32)],
        compiler_params=_params(("parallel", "parallel", "arbitrary")),
        name="conv_ffn",
    )(x, x, x, pre, w_in, w_in, conv_w, conv_w, conv_b, conv_b, w_out, post)


def _rot_half_cols(w, half):
    shp = w.shape
    w = w.reshape(shp[:-1] + (shp[-1] // (2 * half), 2, half))
    return jnp.concatenate([-w[..., 1:2, :], w[..., 0:1, :]], axis=-2).reshape(shp)


def _swap_half(g, half):
    shp = g.shape
    g = g.reshape(shp[:-1] + (shp[-1] // (2 * half), 2, half))
    return jnp.concatenate([g[..., 1:2, :], g[..., 0:1, :]], axis=-2).reshape(shp)


def _pack_w_in(w_in):
    sizes = (512, 512, 512, Q_LORA, KV_LORA, ROPE_B, D_INNER, D_INNER + 2 * N_GROUPS * D_STATE, H_C, H_C,
             H_D * DH_D, KV_D * DH_D, KV_D * DH_D, N_BRANCH * D_MODEL)
    offs = np.concatenate([[0], np.cumsum(sizes)])
    (a_q, a_k, a_v, b_cq, b_ckv, b_kr, c_z, c_xbc, c_dtf, c_dtb, d_q, d_k, d_v, g) = [
        w_in[:, offs[i]:offs[i + 1]] for i in range(len(sizes))]
    d = w_in.shape[0]
    z = lambda n: jnp.zeros((d, n), w_in.dtype)
    kr_blk = jnp.concatenate([z(NOPE_B), b_kr, z(LANE - NOPE_B - ROPE_B)], axis=1)
    krr_blk = jnp.concatenate([z(NOPE_B), _rot_half_cols(b_kr, ROPE_B // 2), z(LANE - NOPE_B - ROPE_B)], axis=1)
    dt_blk = jnp.concatenate([c_dtf, c_dtb, z(LANE - 2 * H_C)], axis=1)
    cols = [
        g,
        a_q * (DH_A ** -0.5 * LOG2E), a_k, a_v,
        c_z, c_xbc[:, :D_INNER],
        d_q, _rot_half_cols(d_q, DH_D // 4),
        b_cq, kr_blk, b_ckv, c_xbc[:, D_INNER:],
        krr_blk, dt_blk,
        d_k, _rot_half_cols(d_k, DH_D // 4), d_v,
        z((N_COLBLK - COL_DV - 1) * LANE),
    ]
    return jnp.concatenate(cols, axis=1).astype(BF16)


def _pad_heads(w, n_heads, per_head, take_from, take, place_at):
    d = w.shape[0]
    w = w.reshape(d, n_heads, per_head)[:, :, take_from:take_from + take]
    w = jnp.pad(w, ((0, 0), (0, 0), (place_at, LANE - place_at - take)))
    return w.reshape(d, n_heads * LANE)


def _rope_tables(l):
    half = ROPE_B // 2
    freqs = ROPE_THETA ** (-jnp.arange(half, dtype=F32) / half)
    pos = jnp.arange(l, dtype=F32)

    def cs(p):
        ang = p[:, None] * freqs[None, :]
        return jnp.cos(ang), jnp.sin(ang)

    c, s = cs(pos)
    ones = jnp.ones((l, NOPE_B), F32)
    zeros = jnp.zeros((l, LANE - NOPE_B - ROPE_B), F32)
    cos_b = jnp.concatenate([ones, c, c, zeros], axis=1)
    sin_b = jnp.concatenate([0 * ones, s, s, zeros], axis=1)
    cr, sr = cs(jnp.floor(pos / GRID_W))
    cc, sc = cs(pos - GRID_W * jnp.floor(pos / GRID_W))
    cos_d = jnp.concatenate([cr, cr, cc, cc] * 2, axis=1)
    sin_d = jnp.concatenate([sr, sr, sc, sc] * 2, axis=1)
    return cos_b, sin_b, cos_d, sin_d


def _layer(x, kv_mem, lp, layer_idx, tables):
    b, l, d = x.shape
    cos_b, sin_b, cos_d, sin_d = tables
    tm = min(512, l)
    t = b * l

    u = _norm_matmul(x.reshape(t, d), lp["norm_pre"][0], lp["w_in_packed"], F32, tm, PROJ_TN)
    u = u.reshape(b, l, N_COLBLK * LANE)

    tq = min(1024, l)
    tk = min(512, l // 2)
    qt_a, k_a, vt_a = _prep_a(u, b, l, tm)
    o_a = _flash(qt_a, k_a, vt_a, lp["alibi"], tq=tq, tk=tk, alibi=True)
    qt_b, k_b, vt_b = _prep_b(u, b, l, tm, cos_b, sin_b, lp["mla_q_norm"].reshape(1, -1),
                              lp["mla_kv_norm"].reshape(1, -1), lp["wq_b"], lp["wqr_b"], lp["wk_b"], lp["wv_b"])
    o_b = _flash(qt_b, k_b, vt_b, lp["alibi"], tq=tq, tk=tk, alibi=False)
    qt_d, k_d, vt_d = _prep_d(u, b, l, tm, cos_d, sin_d, lp["gq_d"], lp["gqr_d"], lp["gk_d"], lp["gkr_d"])
    o_d = _flash(qt_d, k_d, vt_d, lp["alibi"], tq=tq, tk=tk, alibi=False)

    xs = _conv_silu(u, b, l, tm, COL_CXS // 4, D_INNER, lp["ssm_conv_w"][:, :D_INNER], lp["ssm_conv_b"][None, :D_INNER])
    bc = _conv_silu(u, b, l, tm, COL_CBC // 2, 2 * LANE, lp["ssm_conv_w"][:, D_INNER:], lp["ssm_conv_b"][None, D_INNER:])
    y_f = _ssd_scan(xs, bc, u, b, l, lp["alog_vec"], lp["dtb_vec"], reverse=False)
    y_c = _ssd_scan(xs, bc, u, b, l, lp["alog_vec"], lp["dtb_vec"], reverse=True,
                    final_args=(y_f, lp["dskip_x"], lp["ssm_norm"].reshape(1, -1)))

    lam_init = 0.8 - 0.6 * math.exp(-0.3 * layer_idx)
    x = _merge(x, o_a, o_b, y_c, o_d, u, lp["diff_lambda"], lp["diff_norm"].reshape(1, -1), lp["w_branch16"],
               lp["w_out16"], lp["norm_post"][0].reshape(1, -1), lam_init, min(256, l))

    x = _mem_attn(x, kv_mem, lp["norm_pre"][1].reshape(1, -1), lp["w_mem_q16"], lp["w_mem_o16"],
                  lp["norm_post"][1].reshape(1, -1), tm)

    x = _ffn(x, lp["norm_pre"][2].reshape(1, -1), lp["w_ffn_in16"], lp["ffn_conv_w"], lp["ffn_conv_b"].reshape(1, -1),
             lp["w_ffn_out16"], lp["norm_post"][2].reshape(1, -1), min(1024, l), 256)
    return x


def _prep_layer_params(p, layer_idx):
    lp = {name: arr[layer_idx] for name, arr in p.items()}
    lp["w_in_packed"] = _pack_w_in(lp["w_in"])
    qscale_b = (NOPE_B + ROPE_B) ** -0.5 * LOG2E
    wq = lp["w_mla_uq"] * qscale_b
    per_q = NOPE_B + ROPE_B
    lp["wq_b"] = _pad_heads(wq, H_B, per_q, 0, per_q, 0).astype(BF16)
    wq_rope = wq.reshape(Q_LORA, H_B, per_q)[:, :, NOPE_B:]
    lp["wqr_b"] = _pad_heads(_rot_half_cols(wq_rope, ROPE_B // 2).reshape(Q_LORA, H_B * ROPE_B), H_B, ROPE_B, 0, ROPE_B,
                             NOPE_B).astype(BF16)
    per_kv = NOPE_B + V_B
    lp["wk_b"] = _pad_heads(lp["w_mla_ukv"], H_B, per_kv, 0, NOPE_B, 0).astype(BF16)
    lp["wv_b"] = _pad_heads(lp["w_mla_ukv"], H_B, per_kv, NOPE_B, V_B, 0).astype(BF16)
    tile2 = lambda g: jnp.tile(g, 2).reshape(1, LANE)
    lp["gq_d"] = tile2(lp["gqa_q_norm"])
    lp["gqr_d"] = tile2(_swap_half(lp["gqa_q_norm"], DH_D // 4))
    lp["gk_d"] = tile2(lp["gqa_k_norm"])
    lp["gkr_d"] = tile2(_swap_half(lp["gqa_k_norm"], DH_D // 4))
    slopes = np.array([2.0 ** (-8.0 * (i + 1) / H_A) for i in range(H_A)], np.float32) * np.float32(LOG2E)
    lp["alibi"] = jnp.asarray(np.repeat(slopes, 2))
    pad_vec = lambda v: jnp.pad(v.reshape(-1), (0, LANE - 2 * H_C)).reshape(1, LANE)
    lp["alog_vec"] = pad_vec(lp["ssm_a_log"])
    lp["dtb_vec"] = pad_vec(lp["ssm_dt_bias"])
    lp["dskip_x"] = jnp.repeat(lp["ssm_d"], SSM_HEAD).reshape(1, D_INNER)
    lp["w_branch16"] = lp["w_branch"].astype(BF16)
    lp["w_out16"] = lp["w_out"].astype(BF16)
    lp["w_mem_q16"] = (lp["w_mem_q"] * (DH_MEM ** -0.5 * LOG2E)).astype(BF16)
    lp["w_mem_kv16"] = lp["w_mem_kv"].astype(BF16)
    lp["w_mem_o16"] = lp["w_mem_o"].astype(BF16)
    lp["w_ffn_in16"] = lp["w_ffn_in"].astype(BF16)
    lp["w_ffn_out16"] = lp["w_ffn_out"].astype(BF16)
    return lp


def _trunk(x, mem, layer_params):
    b, l, d = x.shape
    m = mem.shape[1]
    tables = _rope_tables(l)
    for layer_idx, lp in enumerate(layer_params):
        kv = _norm_matmul(mem.reshape(b * m, d), lp["mem_norm"], lp["w_mem_kv16"], BF16, min(256, b * m), 512)
        x = _layer(x, kv.reshape(b, m, 2 * H_MEM * DH_MEM), lp, layer_idx, tables)
    return x


def kernel(x_prompt, x_sample, mem_prompt, mem_sample, w_in, w_branch, w_out, diff_lambda, diff_norm, mla_q_norm,
           mla_kv_norm, w_mla_uq, w_mla_ukv, ssm_conv_w, ssm_conv_b, ssm_a_log, ssm_dt_bias, ssm_d, ssm_norm,
           gqa_q_norm, gqa_k_norm, mem_norm, w_mem_q, w_mem_kv, w_mem_o, w_ffn_in, ffn_conv_w, ffn_conv_b, w_ffn_out,
           norm_pre, norm_post):
    p = {
        "w_in": w_in, "w_branch": w_branch, "w_out": w_out, "diff_lambda": diff_lambda, "diff_norm": diff_norm,
        "mla_q_norm": mla_q_norm, "mla_kv_norm": mla_kv_norm, "w_mla_uq": w_mla_uq, "w_mla_ukv": w_mla_ukv,
        "ssm_conv_w": ssm_conv_w, "ssm_conv_b": ssm_conv_b, "ssm_a_log": ssm_a_log, "ssm_dt_bias": ssm_dt_bias,
        "ssm_d": ssm_d, "ssm_norm": ssm_norm, "gqa_q_norm": gqa_q_norm, "gqa_k_norm": gqa_k_norm,
        "mem_norm": mem_norm, "w_mem_q": w_mem_q, "w_mem_kv": w_mem_kv, "w_mem_o": w_mem_o,
        "w_ffn_in": w_ffn_in, "ffn_conv_w": ffn_conv_w, "ffn_conv_b": ffn_conv_b, "w_ffn_out": w_ffn_out,
        "norm_pre": norm_pre, "norm_post": norm_post,
    }
    layer_params = [_prep_layer_params(p, i) for i in range(DEPTH)]
    y_prompt = _trunk(x_prompt, mem_prompt, layer_params)
    y_sample = _trunk(x_sample, mem_sample, layer_params)
    return (y_prompt, y_sample)
```

```python
import functools
import math

import jax
import jax.numpy as jnp
import numpy as np
from jax import lax
from jax.experimental import pallas as pl
from jax.experimental.pallas import tpu as pltpu

F32 = jnp.float32
BF16 = jnp.bfloat16
HIGHEST = lax.Precision.HIGHEST

D_MODEL = 1024
DEPTH = 2
GRID_W = 64
ROPE_THETA = 10000.0
EPS = 1e-6
N_BRANCH = 4
BRANCH_W = 512
H_A = 4
DH_A = 64
H_B = 4
Q_LORA = 384
KV_LORA = 256
NOPE_B = 64
ROPE_B = 32
V_B = 128
D_INNER = 512
SSM_HEAD = 64
H_C = D_INNER // SSM_HEAD
N_GROUPS = 2
D_STATE = 64
CHUNK = 128
H_D = 8
KV_D = 2
DH_D = 64
H_MEM = 4
DH_MEM = 128
D_FF = 2816

LANE = 128
LOG2E = 1.4426950408889634
NEG_BIG = -1e30
ONES_ROWS = 16
VMEM_LIMIT = 56 * 1024 * 1024

COL_G = 0
COL_AQ = 32
COL_AK = 36
COL_AV = 40
COL_CZ = 44
COL_CXS = 48
COL_DQ = 52
COL_DQR = 56
COL_BCQ = 60
COL_BKR = 63
COL_BCKV = 64
COL_CBC = 66
COL_BKRR = 68
COL_CDT = 69
COL_DK = 70
COL_DKR = 71
COL_DV = 72
N_COLBLK = 76
PROJ_TN = (N_COLBLK // 4) * LANE


def _params(sem, flags=None):
    return pltpu.CompilerParams(dimension_semantics=sem, vmem_limit_bytes=VMEM_LIMIT, flags=flags)


def _rms(x, g):
    return x * lax.rsqrt(jnp.mean(x * x, axis=-1, keepdims=True) + EPS) * g


def _norm_matmul_kernel(x_ref, g_ref, w_ref, o_ref, xn_ref):
    @pl.when(pl.program_id(1) == 0)
    def _():
        xn_ref[...] = _rms(x_ref[...], g_ref[...]).astype(BF16)

    o_ref[...] = jnp.dot(xn_ref[...], w_ref[...], preferred_element_type=F32).astype(o_ref.dtype)


def _norm_matmul(x, g, w, out_dtype, tm, tn):
    t, d = x.shape
    n = w.shape[1]
    return pl.pallas_call(
        _norm_matmul_kernel,
        grid=(t // tm, n // tn),
        in_specs=[
            pl.BlockSpec((tm, d), lambda i, j: (i, 0)),
            pl.BlockSpec((1, d), lambda i, j: (0, 0)),
            pl.BlockSpec((d, tn), lambda i, j: (0, j)),
        ],
        out_specs=pl.BlockSpec((tm, tn), lambda i, j: (i, j)),
        out_shape=jax.ShapeDtypeStruct((t, n), out_dtype),
        scratch_shapes=[pltpu.VMEM((tm, d), BF16)],
        compiler_params=_params(("parallel", "arbitrary")),
        name="norm_matmul",
    )(x, g.reshape(1, d), w)


def _ones_rows(tm):
    row = lax.broadcasted_iota(jnp.int32, (ONES_ROWS, tm), 0)
    return jnp.where(row == 0, 1.0, 0.0).astype(BF16)


def _bf16_terms(x, n):
    terms = []
    for _ in range(n):
        t = float(np.asarray(x, np.float32).astype(BF16).astype(np.float32))
        terms.append(t)
        x = x - t
    return terms


LOG2E_TERMS = _bf16_terms(LOG2E, 3)
POS_SPLIT = 32


def _alibi_features(idx, along, slope, key_side):
    slot = lax.broadcasted_iota(jnp.int32, idx.shape, along)
    lo_i = jnp.bitwise_and(idx, POS_SPLIT - 1)
    lo = lo_i.astype(F32)
    hi = (idx - lo_i).astype(F32)
    t = jnp.where(slot >= 9, slot - 9, jnp.where(slot >= 6, slot - 6, jnp.where(slot >= 3, slot - 3, slot)))
    c = slope * jnp.where(t == 0, LOG2E_TERMS[0], jnp.where(t == 1, LOG2E_TERMS[1], LOG2E_TERMS[2]))
    if key_side:
        f = jnp.where(slot < 3, lo, jnp.where(slot < 6, hi, jnp.where(slot < 12, c, 0.0)))
    else:
        f = jnp.where(slot < 6, c, jnp.where(slot < 9, -lo, jnp.where(slot < 12, -hi, 0.0)))
    return f


def _prep_a_kernel(slope_ref, q_ref, k_ref, v_ref, qt_ref, ko_ref, vt_ref, *, tq, tk):
    q = q_ref[0].astype(F32)
    tm = q.shape[0]
    row0 = pl.program_id(1) * tm
    slope = slope_ref[pl.program_id(2)]
    lane = lax.broadcasted_iota(jnp.int32, q.shape, 1)
    q1 = jnp.where(lane < DH_A, q, 0.0)
    j = jnp.bitwise_and(row0 + lax.broadcasted_iota(jnp.int32, (LANE, tm), 1), tq - 1)
    fq = _alibi_features(j, 0, slope, key_side=False).astype(BF16)
    qt_ref[0, 0, 0:LANE, :] = q1.T.astype(BF16)
    qt_ref[0, 0, LANE:, :] = fq
    qt_ref[0, 1, 0:LANE, :] = (q - q1).T.astype(BF16)
    qt_ref[0, 1, LANE:, :] = fq
    i = jnp.bitwise_and(row0 + lax.broadcasted_iota(jnp.int32, (tm, LANE), 0), tk - 1)
    ko_ref[0, 0, :, 0:LANE] = k_ref[0].astype(BF16)
    ko_ref[0, 0, :, LANE:] = _alibi_features(i, 1, slope, key_side=True).astype(BF16)
    vt_ref[0, 0, 0:2 * DH_A, :] = v_ref[0].astype(F32).T.astype(BF16)
    vt_ref[0, 0, 2 * DH_A:, :] = _ones_rows(tm)


def _prep_a(u, slopes, b, l, tm, tq, tk):
    dv = 2 * DH_A + ONES_ROWS
    return pl.pallas_call(
        functools.partial(_prep_a_kernel, tq=tq, tk=tk),
        grid=(b, l // tm, H_A),
        in_specs=[
            pl.BlockSpec(memory_space=pltpu.SMEM),
            pl.BlockSpec((1, tm, LANE), lambda bi, i, h: (bi, i, COL_AQ + h)),
            pl.BlockSpec((1, tm, LANE), lambda bi, i, h: (bi, i, COL_AK + h)),
            pl.BlockSpec((1, tm, LANE), lambda bi, i, h: (bi, i, COL_AV + h)),
        ],
        out_specs=[
            pl.BlockSpec((1, 2, 2 * LANE, tm), lambda bi, i, h: (bi, h, 0, i)),
            pl.BlockSpec((1, 1, tm, 2 * LANE), lambda bi, i, h: (bi, h, i, 0)),
            pl.BlockSpec((1, 1, dv, tm), lambda bi, i, h: (bi, h, 0, i)),
        ],
        out_shape=[
            jax.ShapeDtypeStruct((b, 2 * H_A, 2 * LANE, l), BF16),
            jax.ShapeDtypeStruct((b, H_A, l, 2 * LANE), BF16),
            jax.ShapeDtypeStruct((b, H_A, dv, l), BF16),
        ],
        compiler_params=_params(("parallel", "parallel", "parallel")),
        name="prep_diff",
    )(slopes, u, u, u)


def _prep_b_kernel(cq_ref, ckv_ref, kr_ref, krr_ref, cos_ref, sin_ref, gq_ref, gkv_ref,
                   wq_ref, wqr_ref, wk_ref, wv_ref, qt_ref, ko_ref, vt_ref):
    tm = cq_ref.shape[1]
    cos = cos_ref[...]
    sin = sin_ref[...]
    cqn = _rms(cq_ref[0].astype(F32), gq_ref[...]).astype(BF16)
    ckn = _rms(ckv_ref[0].astype(F32), gkv_ref[...]).astype(BF16)
    qa = jnp.dot(cqn, wq_ref[...], preferred_element_type=F32)
    qb = jnp.dot(cqn, wqr_ref[...], preferred_element_type=F32)
    kn = jnp.dot(ckn, wk_ref[...], preferred_element_type=F32)
    vv = jnp.dot(ckn, wv_ref[...], preferred_element_type=F32)
    krope = kr_ref[0].astype(F32) * cos + krr_ref[0].astype(F32) * sin
    ones = _ones_rows(tm)
    for h in range(H_B):
        sl = slice(h * LANE, (h + 1) * LANE)
        qh = qa[:, sl] * cos + qb[:, sl] * sin
        qt_ref[0, h] = qh.T.astype(BF16)
        ko_ref[0, h] = (kn[:, sl] + krope).astype(BF16)
        vt_ref[0, h, 0:V_B, :] = vv[:, sl].T.astype(BF16)
        vt_ref[0, h, V_B:, :] = ones


def _prep_b(u, b, l, tm, cos, sin, gq, gkv, wq, wqr, wk, wv):
    dv = V_B + ONES_ROWS
    const = lambda shape: pl.BlockSpec(shape, lambda bi, i: (0,) * len(shape))
    return pl.pallas_call(
        _prep_b_kernel,
        grid=(b, l // tm),
        in_specs=[
            pl.BlockSpec((1, tm, Q_LORA), lambda bi, i: (bi, i, COL_BCQ // 3)),
            pl.BlockSpec((1, tm, KV_LORA), lambda bi, i: (bi, i, COL_BCKV // 2)),
            pl.BlockSpec((1, tm, LANE), lambda bi, i: (bi, i, COL_BKR)),
            pl.BlockSpec((1, tm, LANE), lambda bi, i: (bi, i, COL_BKRR)),
            pl.BlockSpec((tm, LANE), lambda bi, i: (i, 0)),
            pl.BlockSpec((tm, LANE), lambda bi, i: (i, 0)),
            const((1, Q_LORA)), const((1, KV_LORA)),
            const((Q_LORA, H_B * LANE)), const((Q_LORA, H_B * LANE)),
            const((KV_LORA, H_B * LANE)), const((KV_LORA, H_B * LANE)),
        ],
        out_specs=[
            pl.BlockSpec((1, H_B, LANE, tm), lambda bi, i: (bi, 0, 0, i)),
            pl.BlockSpec((1, H_B, tm, LANE), lambda bi, i: (bi, 0, i, 0)),
            pl.BlockSpec((1, H_B, dv, tm), lambda bi, i: (bi, 0, 0, i)),
        ],
        out_shape=[
            jax.ShapeDtypeStruct((b, H_B, LANE, l), BF16),
            jax.ShapeDtypeStruct((b, H_B, l, LANE), BF16),
            jax.ShapeDtypeStruct((b, H_B, dv, l), BF16),
        ],
        compiler_params=_params(("parallel", "parallel")),
        name="prep_mla",
    )(u, u, u, u, cos, sin, gq, gkv, wq, wqr, wk, wv)


def _prep_d_kernel(q_ref, qr_ref, k_ref, kr_ref, v_ref, cos_ref, sin_ref, gq_ref, gqr_ref, gk_ref, gkr_ref,
                   qt_ref, ko_ref, vt_ref, *, qscale):
    tm = q_ref.shape[1]
    cos = cos_ref[...]
    sin = sin_ref[...]
    r_i = lax.broadcasted_iota(jnp.int32, (LANE, LANE), 0) // DH_D
    c_i = lax.broadcasted_iota(jnp.int32, (LANE, LANE), 1) // DH_D
    blockdiag = jnp.where(r_i == c_i, 1.0, 0.0).astype(F32)

    def norm_rope(x, xr, g, gr):
        ss = jnp.dot(x * x, blockdiag, precision=HIGHEST, preferred_element_type=F32)
        r = lax.rsqrt(ss * (1.0 / DH_D) + EPS)
        return r * (x * g * cos + xr * gr * sin)

    q = q_ref[0].astype(F32)
    qr = qr_ref[0].astype(F32)
    zeros = jnp.zeros((DH_D, tm), F32)
    for j in range(H_D // 2):
        sl = slice(j * LANE, (j + 1) * LANE)
        y = norm_rope(q[:, sl], qr[:, sl], gq_ref[...], gqr_ref[...]) * qscale
        yt = y.T
        grp = (2 * j) // (H_D // KV_D)
        for e in range(2):
            piece = yt[e * DH_D:(e + 1) * DH_D]
            full = jnp.concatenate([piece, zeros], axis=0) if grp == 0 else jnp.concatenate([zeros, piece], axis=0)
            qt_ref[0, 2 * j + e] = full.astype(BF16)
    ko_ref[0, 0] = norm_rope(k_ref[0].astype(F32), kr_ref[0].astype(F32), gk_ref[...], gkr_ref[...]).astype(BF16)
    vt = v_ref[0].astype(F32).T
    ones = _ones_rows(tm)
    for grp in range(KV_D):
        vt_ref[0, grp, 0:DH_D, :] = vt[grp * DH_D:(grp + 1) * DH_D].astype(BF16)
        vt_ref[0, grp, DH_D:, :] = ones


def _prep_d(u, b, l, tm, cos, sin, gq, gqr, gk, gkr):
    dv = DH_D + ONES_ROWS
    const = lambda shape: pl.BlockSpec(shape, lambda bi, i: (0,) * len(shape))
    return pl.pallas_call(
        functools.partial(_prep_d_kernel, qscale=DH_D ** -0.5 * LOG2E),
        grid=(b, l // tm),
        in_specs=[
            pl.BlockSpec((1, tm, 4 * LANE), lambda bi, i: (bi, i, COL_DQ // 4)),
            pl.BlockSpec((1, tm, 4 * LANE), lambda bi, i: (bi, i, COL_DQR // 4)),
            pl.BlockSpec((1, tm, LANE), lambda bi, i: (bi, i, COL_DK)),
            pl.BlockSpec((1, tm, LANE), lambda bi, i: (bi, i, COL_DKR)),
            pl.BlockSpec((1, tm, LANE), lambda bi, i: (bi, i, COL_DV)),
            pl.BlockSpec((tm, LANE), lambda bi, i: (i, 0)),
            pl.BlockSpec((tm, LANE), lambda bi, i: (i, 0)),
            const((1, LANE)), const((1, LANE)), const((1, LANE)), const((1, LANE)),
        ],
        out_specs=[
            pl.BlockSpec((1, H_D, LANE, tm), lambda bi, i: (bi, 0, 0, i)),
            pl.BlockSpec((1, 1, tm, LANE), lambda bi, i: (bi, 0, i, 0)),
            pl.BlockSpec((1, KV_D, dv, tm), lambda bi, i: (bi, 0, 0, i)),
        ],
        out_shape=[
            jax.ShapeDtypeStruct((b, H_D, LANE, l), BF16),
            jax.ShapeDtypeStruct((b, 1, l, LANE), BF16),
            jax.ShapeDtypeStruct((b, KV_D, dv, l), BF16),
        ],
        compiler_params=_params(("parallel", "parallel")),
        name="prep_gqa",
    )(u, u, u, u, u, cos, sin, gq, gqr, gk, gkr)


def _flash_kernel(slope_ref, qt_ref, k_ref, vt_ref, o_ref, m_ref, acc_ref, sa_ref, sb_ref, pa_ref, pb_ref,
                  *, tk, dv, alibi):
    tq = qt_ref.shape[3]
    l = k_ref.shape[2]
    n_pairs = l // (2 * tk)
    m_ref[...] = jnp.full(m_ref.shape, NEG_BIG, F32)
    acc_ref[...] = jnp.zeros(acc_ref.shape, F32)
    q0 = pl.program_id(2) * tq
    slope = slope_ref[pl.program_id(1)] if alibi else None

    def scores(k0, s_ref, p_ref, diag):
        if not alibi:
            s = jnp.dot(k_ref[0, 0, pl.ds(k0, tk), :], qt_ref[0, 0], preferred_element_type=F32)
        elif diag:
            rel = lax.broadcasted_iota(jnp.int32, (tk, tq), 0) - lax.broadcasted_iota(jnp.int32, (tk, tq), 1)
            s = jnp.dot(k_ref[0, 0, pl.ds(k0, tk), 0:LANE], qt_ref[0, 0, 0:LANE, :], preferred_element_type=F32)
            s = s - slope * jnp.abs(rel + (k0 - q0)).astype(F32)
        else:
            sign = jnp.where(k0 < q0, 1.0, -1.0).astype(BF16)
            kt = jnp.concatenate([k_ref[0, 0, pl.ds(k0, tk), 0:LANE], k_ref[0, 0, pl.ds(k0, tk), LANE:] * sign], axis=1)
            s = jnp.dot(kt, qt_ref[0, 0], preferred_element_type=F32)
        s_ref[...] = s
        p_ref[...] = jnp.max(s.reshape(tk // 8, 8, tq), axis=0)

    def accumulate(s_ref, p_ref, k0, diag):
        off = slope * jnp.abs(k0 - q0).astype(F32) if (alibi and not diag) else 0.0
        m_old = m_ref[...]
        m_new = jnp.maximum(m_old, jnp.max(p_ref[...], axis=0, keepdims=True) - off)
        alpha = jnp.exp2(m_old - m_new)
        p = jnp.exp2((s_ref[...] - (m_new + off)).astype(BF16))
        pv = jnp.dot(vt_ref[0, 0, :, pl.ds(k0, tk)], p, preferred_element_type=F32)
        acc_ref[...] = alpha * acc_ref[...] + pv
        m_ref[...] = m_new

    def pair(j, diag_cur, diag_next):
        k0 = pl.multiple_of(j * (2 * tk), 2 * tk)
        k1 = pl.multiple_of(k0 + tk, tk)
        k2 = pl.multiple_of(jnp.minimum(k0 + 2 * tk, l - tk), tk)
        scores(k1, sb_ref, pb_ref, diag_cur)
        accumulate(sa_ref, pa_ref, k0, diag_cur)
        scores(k2, sa_ref, pa_ref, diag_next)
        accumulate(sb_ref, pb_ref, k1, diag_cur)

    if alibi:
        jd = pl.program_id(2)
        lax.cond(jd == 0, lambda: scores(0, sa_ref, pa_ref, True), lambda: scores(0, sa_ref, pa_ref, False))

        def body(j, carry):
            which = jnp.where(j == jd, 2, jnp.where(j + 1 == jd, 1, 0))
            lax.switch(which, [lambda: pair(j, False, False), lambda: pair(j, False, True),
                               lambda: pair(j, True, False)])
            return carry
    else:
        scores(0, sa_ref, pa_ref, False)

        def body(j, carry):
            pair(j, False, False)
            return carry

    lax.fori_loop(0, n_pairs, body, 0)
    acc = acc_ref[...]
    o_ref[0, 0] = acc[0:dv] / acc[dv:dv + 1]


def _flash(qt, k, vt, slopes, *, tq, tk, alibi):
    b, hq, dk, l = qt.shape
    hk = k.shape[1]
    hv = vt.shape[1]
    dvp = vt.shape[2]
    dv = dvp - ONES_ROWS
    assert l % (2 * tk) == 0 and l % tq == 0 and (not alibi or tq == 2 * tk)
    return pl.pallas_call(
        functools.partial(_flash_kernel, tk=tk, dv=dv, alibi=alibi),
        grid=(b, hq, l // tq),
        in_specs=[
            pl.BlockSpec(memory_space=pltpu.SMEM),
            pl.BlockSpec((1, 1, dk, tq), lambda bi, h, i: (bi, h, 0, i)),
            pl.BlockSpec((1, 1, l, dk), lambda bi, h, i: (bi, h // (hq // hk), 0, 0)),
            pl.BlockSpec((1, 1, dvp, l), lambda bi, h, i: (bi, h // (hq // hv), 0, 0)),
        ],
        out_specs=pl.BlockSpec((1, 1, dv, tq), lambda bi, h, i: (bi, h, 0, i)),
        out_shape=jax.ShapeDtypeStruct((b, hq, dv, l), F32),
        scratch_shapes=[pltpu.VMEM((1, tq), F32), pltpu.VMEM((dvp, tq), F32),
                        pltpu.VMEM((tk, tq), F32), pltpu.VMEM((tk, tq), F32),
                        pltpu.VMEM((8, tq), F32), pltpu.VMEM((8, tq), F32)],
        compiler_params=_params(("parallel", "parallel", "parallel")),
        name="flash_" + ("alibi" if alibi else "plain") + f"_{hq}_{dv}",
    )(slopes, qt, k, vt)


def _conv3_rows(xe, w_ref, b_ref, tm):
    n = xe.shape[0]
    halo = (n - tm) // 2
    up = pltpu.roll(xe, 1, 0)
    dn = pltpu.roll(xe, n - 1, 0)
    y = up * w_ref[0:1, :] + xe * w_ref[1:2, :] + dn * w_ref[2:3, :] + b_ref[...]
    return y[halo:halo + tm]


def _halo_rows(prev_ref, main_ref, next_ref, i, n_i):
    prev = jnp.where(i == 0, 0.0, prev_ref[0].astype(F32))
    nxt = jnp.where(i == n_i - 1, 0.0, next_ref[0].astype(F32))
    return jnp.concatenate([prev, main_ref[0].astype(F32), nxt], axis=0)


def _conv_silu_kernel(p_ref, x_ref, n_ref, w_ref, b_ref, o_ref):
    tm = x_ref.shape[1]
    xe = _halo_rows(p_ref, x_ref, n_ref, pl.program_id(1), pl.num_programs(1))
    y = _conv3_rows(xe, w_ref, b_ref, tm)
    o_ref[0] = y * (1.0 / (1.0 + jnp.exp(-y)))


BF16_ROWS = 16


def _conv_silu(u, b, l, tm, col_blk, width, w, bias):
    r8 = tm // BF16_ROWS
    n8 = l // BF16_ROWS
    return pl.pallas_call(
        _conv_silu_kernel,
        grid=(b, l // tm),
        in_specs=[
            pl.BlockSpec((1, BF16_ROWS, width), lambda bi, i: (bi, jnp.maximum(i * r8 - 1, 0), col_blk)),
            pl.BlockSpec((1, tm, width), lambda bi, i: (bi, i, col_blk)),
            pl.BlockSpec((1, BF16_ROWS, width), lambda bi, i: (bi, jnp.minimum((i + 1) * r8, n8 - 1), col_blk)),
            pl.BlockSpec((3, width), lambda bi, i: (0, 0)),
            pl.BlockSpec((1, width), lambda bi, i: (0, 0)),
        ],
        out_specs=pl.BlockSpec((1, tm, width), lambda bi, i: (bi, i, 0)),
        out_shape=jax.ShapeDtypeStruct((b, l, width), F32),
        compiler_params=_params(("parallel", "parallel")),
        name=f"conv_silu_{width}",
    )(u, u, u, w, bias)


def _ssd_kernel(*refs, reverse, final):
    if final:
        (xs_ref, bc_b_ref, bc_c_ref, dt_ref, alog_ref, dtb_ref, yprev_ref, z_ref, dskip_ref, norm_ref,
         y_ref, state_ref) = refs
    else:
        xs_ref, bc_b_ref, bc_c_ref, dt_ref, alog_ref, dtb_ref, y_ref, state_ref = refs
    q = CHUNK

    @pl.when(pl.program_id(1) == 0)
    def _():
        state_ref[...] = jnp.zeros(state_ref.shape, F32)

    row = lax.broadcasted_iota(jnp.int32, (q, q), 0)
    col = lax.broadcasted_iota(jnp.int32, (q, q), 1)
    mask = (col >= row) if reverse else (col <= row)
    tri = jnp.where(mask, 1.0, 0.0).astype(F32)
    head_off = H_C if reverse else 0

    x = dt_ref[0].astype(F32) + dtb_ref[...]
    dt = jnp.maximum(x, 0.0) + jnp.log(1.0 + jnp.exp(-jnp.abs(x)))
    a = dt * (-jnp.exp(alog_ref[...]))
    cs = jnp.dot(tri, a, precision=HIGHEST, preferred_element_type=F32)
    e_r = lax.broadcasted_iota(jnp.int32, (q, D_INNER), 0)
    e_c = lax.broadcasted_iota(jnp.int32, (q, D_INNER), 1) // SSM_HEAD + head_off
    expand = jnp.where(e_r == e_c, 1.0, 0.0).astype(F32)
    dt_x = jnp.dot(dt, expand, precision=HIGHEST, preferred_element_type=F32)
    cs_x = jnp.dot(cs, expand, precision=HIGHEST, preferred_element_type=F32)
    tot_x = cs_x[0:1] if reverse else cs_x[q - 1:q]

    xs = xs_ref[0]
    xdt = xs * dt_x
    w_in = (xdt * jnp.exp(tot_x - cs_x)).astype(BF16)
    bt = bc_b_ref[0].T
    bt16 = bt.astype(BF16)
    cm = bc_c_ref[0]
    cs_t = cs.T
    lane = lax.broadcasted_iota(jnp.int32, (q, LANE), 1)
    rowi = lax.broadcasted_iota(jnp.int32, (q, LANE), 0)

    outs = []
    for pair in range(H_C // 2):
        grp = (2 * pair) // (H_C // N_GROUPS)
        in_grp = (lane // D_STATE) == grp
        c_g = jnp.where(in_grp, cm, 0.0).astype(BF16)
        cb = jnp.dot(c_g, bt16, preferred_element_type=F32)
        psl = slice(pair * LANE, (pair + 1) * LANE)
        x_p = xdt[:, psl].astype(BF16)
        ys = []
        for e in range(2):
            hl = head_off + 2 * pair + e
            seg = cs[:, hl:hl + 1] - cs_t[hl:hl + 1, :]
            dec = jnp.exp(jnp.where(mask, seg, -jnp.inf))
            ys.append(jnp.dot((cb * dec).astype(BF16), x_p, preferred_element_type=F32))
        y_diag = jnp.where(lane < SSM_HEAD, ys[0], ys[1])
        st = state_ref[pair]
        y_off = jnp.dot(c_g, st.astype(BF16), preferred_element_type=F32) * jnp.exp(cs_x[:, psl])
        outs.append(y_diag + y_off)
        upd = jnp.dot(bt16, w_in[:, psl], preferred_element_type=F32)
        keep = (rowi // D_STATE) == grp
        state_ref[pair] = jnp.where(keep, st * jnp.exp(tot_x[:, psl]) + upd, 0.0)
    y = jnp.concatenate(outs, axis=1)
    if final:
        y = y + yprev_ref[0] + dskip_ref[...] * xs
        z = z_ref[0].astype(F32)
        y = y * (z * (1.0 / (1.0 + jnp.exp(-z))))
        y = _rms(y, norm_ref[...])
    y_ref[0] = y


def _ssd_scan(xs, bc, u, b, l, alog, dtb, reverse, final_args=None):
    nc = l // CHUNK
    cidx = (lambda c: nc - 1 - c) if reverse else (lambda c: c)
    in_specs = [
        pl.BlockSpec((1, CHUNK, D_INNER), lambda bi, c: (bi, cidx(c), 0)),
        pl.BlockSpec((1, CHUNK, LANE), lambda bi, c: (bi, cidx(c), 0)),
        pl.BlockSpec((1, CHUNK, LANE), lambda bi, c: (bi, cidx(c), 1)),
        pl.BlockSpec((1, CHUNK, LANE), lambda bi, c: (bi, cidx(c), COL_CDT)),
        pl.BlockSpec((1, LANE), lambda bi, c: (0, 0)),
        pl.BlockSpec((1, LANE), lambda bi, c: (0, 0)),
    ]
    args = [xs, bc, bc, u, alog, dtb]
    final = final_args is not None
    if final:
        yprev, dskip, norm = final_args
        in_specs += [
            pl.BlockSpec((1, CHUNK, D_INNER), lambda bi, c: (bi, cidx(c), 0)),
            pl.BlockSpec((1, CHUNK, D_INNER), lambda bi, c: (bi, cidx(c), COL_CZ // 4)),
            pl.BlockSpec((1, D_INNER), lambda bi, c: (0, 0)),
            pl.BlockSpec((1, D_INNER), lambda bi, c: (0, 0)),
        ]
        args += [yprev, u, dskip, norm]
    return pl.pallas_call(
        functools.partial(_ssd_kernel, reverse=reverse, final=final),
        grid=(b, nc),
        in_specs=in_specs,
        out_specs=pl.BlockSpec((1, CHUNK, D_INNER), lambda bi, c: (bi, cidx(c), 0)),
        out_shape=jax.ShapeDtypeStruct((b, l, D_INNER), F32),
        scratch_shapes=[pltpu.VMEM((H_C // 2, LANE, LANE), F32)],
        compiler_params=_params(("parallel", "arbitrary")),
        name="ssd_bwd" if reverse else "ssd_fwd",
    )(*args)


def _merge_kernel(x_ref, oa_ref, ob_ref, yc_ref, od_ref, g_ref, lam_ref, subln_ref, wb_ref, wo_ref, post_ref,
                  o_ref, *, lam_init):
    lp = lam_ref[...]
    lam = (jnp.exp(jnp.sum(lp[0:1] * lp[1:2], axis=-1, keepdims=True))
           - jnp.exp(jnp.sum(lp[2:3] * lp[3:4], axis=-1, keepdims=True)) + lam_init)
    ya = []
    for h in range(H_A):
        o = (oa_ref[0, 2 * h] - lam * oa_ref[0, 2 * h + 1]).T
        ya.append(_rms(o, subln_ref[...]) * (1.0 - lam_init))
    ya = jnp.concatenate(ya, axis=1).astype(BF16)
    yb = jnp.concatenate([ob_ref[0, h].T for h in range(H_B)], axis=1).astype(BF16)
    yc = yc_ref[0].astype(BF16)
    yd = jnp.concatenate([jnp.concatenate([od_ref[0, 2 * j], od_ref[0, 2 * j + 1]], axis=0).T
                          for j in range(H_D // 2)], axis=1).astype(BF16)
    merged = None
    for n, y in enumerate((ya, yb, yc, yd)):
        proj = jnp.dot(y, wb_ref[n], preferred_element_type=F32)
        gn = g_ref[0, :, n * D_MODEL:(n + 1) * D_MODEL].astype(F32)
        term = proj * (1.0 / (1.0 + jnp.exp(-gn)))
        merged = term if merged is None else merged + term
    out = jnp.dot(merged.astype(BF16), wo_ref[...], preferred_element_type=F32)
    o_ref[0] = x_ref[0] + _rms(out, post_ref[...])


def _merge(x, oa, ob, yc, od, u, lam_p, subln, wb, wo, post, lam_init, tm):
    b, l, d = x.shape
    const = lambda shape: pl.BlockSpec(shape, lambda bi, i: (0,) * len(shape))
    return pl.pallas_call(
        functools.partial(_merge_kernel, lam_init=lam_init),
        grid=(b, l // tm),
        in_specs=[
            pl.BlockSpec((1, tm, d), lambda bi, i: (bi, i, 0)),
            pl.BlockSpec((1, 2 * H_A, LANE, tm), lambda bi, i: (bi, 0, 0, i)),
            pl.BlockSpec((1, H_B, V_B, tm), lambda bi, i: (bi, 0, 0, i)),
            pl.BlockSpec((1, tm, D_INNER), lambda bi, i: (bi, i, 0)),
            pl.BlockSpec((1, H_D, DH_D, tm), lambda bi, i: (bi, 0, 0, i)),
            pl.BlockSpec((1, tm, N_BRANCH * d), lambda bi, i: (bi, i, COL_G // 32)),
            const((4, DH_A)), const((1, 2 * DH_A)),
            const((N_BRANCH, BRANCH_W, d)), const((d, d)), const((1, d)),
        ],
        out_specs=pl.BlockSpec((1, tm, d), lambda bi, i: (bi, i, 0)),
        out_shape=jax.ShapeDtypeStruct((b, l, d), F32),
        compiler_params=_params(("parallel", "parallel")),
        name="merge",
    )(x, oa, ob, yc, od, u, lam_p, subln, wb, wo, post)


def _mem_attn_kernel(x_ref, kv_ref, pre_ref, wq_ref, wo_ref, post_ref, o_ref):
    x = x_ref[0]
    xn = _rms(x, pre_ref[...]).astype(BF16)
    q = jnp.dot(xn, wq_ref[...], preferred_element_type=F32).astype(BF16)
    outs = []
    for h in range(H_MEM):
        k = kv_ref[0, :, 2 * h * DH_MEM:(2 * h + 1) * DH_MEM]
        v = kv_ref[0, :, (2 * h + 1) * DH_MEM:(2 * h + 2) * DH_MEM]
        s = lax.dot_general(q[:, h * DH_MEM:(h + 1) * DH_MEM], k, (((1,), (1,)), ((), ())),
                            preferred_element_type=F32)
        p = jnp.exp2(s - jnp.max(s, axis=-1, keepdims=True))
        o = jnp.dot(p.astype(BF16), v, preferred_element_type=F32)
        outs.append(o / jnp.sum(p, axis=-1, keepdims=True))
    o = jnp.concatenate(outs, axis=1).astype(BF16)
    y = jnp.dot(o, wo_ref[...], preferred_element_type=F32)
    o_ref[0] = x + _rms(y, post_ref[...])


def _mem_attn(x, kv, pre, wq, wo, post, tm):
    b, l, d = x.shape
    m = kv.shape[1]
    const = lambda shape: pl.BlockSpec(shape, lambda bi, i: (0,) * len(shape))
    return pl.pallas_call(
        _mem_attn_kernel,
        grid=(b, l // tm),
        in_specs=[
            pl.BlockSpec((1, tm, d), lambda bi, i: (bi, i, 0)),
            pl.BlockSpec((1, m, 2 * H_MEM * DH_MEM), lambda bi, i: (bi, 0, 0)),
            const((1, d)), const((d, H_MEM * DH_MEM)), const((H_MEM * DH_MEM, d)), const((1, d)),
        ],
        out_specs=pl.BlockSpec((1, tm, d), lambda bi, i: (bi, i, 0)),
        out_shape=jax.ShapeDtypeStruct((b, l, d), F32),
        compiler_params=_params(("parallel", "parallel")),
        name="mem_attn",
    )(x, kv, pre, wq, wo, post)


def _ffn_kernel(p_ref, x_ref, n_ref, pre_ref, wa_ref, wg_ref, cwa_ref, cwg_ref, cba_ref, cbg_ref, wo_ref, post_ref,
                o_ref, xe_ref, acc_ref):
    tm = x_ref.shape[1]
    c = pl.program_id(2)

    @pl.when(c == 0)
    def _():
        xe = _halo_rows(p_ref, x_ref, n_ref, pl.program_id(1), pl.num_programs(1))
        xe_ref[...] = _rms(xe, pre_ref[...]).astype(BF16)
        acc_ref[...] = jnp.zeros(acc_ref.shape, F32)

    xe = xe_ref[...]
    a = _conv3_rows(jnp.dot(xe, wa_ref[...], preferred_element_type=F32), cwa_ref, cba_ref, tm)
    g = _conv3_rows(jnp.dot(xe, wg_ref[...], preferred_element_type=F32), cwg_ref, cbg_ref, tm)
    act = 0.5 * a * (1.0 + lax.erf(a * (2.0 ** -0.5))) * g
    acc_ref[...] += jnp.dot(act.astype(BF16), wo_ref[...], preferred_element_type=F32)

    @pl.when(c == pl.num_programs(2) - 1)
    def _():
        o_ref[0] = x_ref[0] + _rms(acc_ref[...], post_ref[...])


def _ffn(x, pre, w_in, conv_w, conv_b, w_out, post, tm, tf):
    b, l, d = x.shape
    nf = D_FF // tf
    r8 = tm // 8
    n8 = l // 8
    return pl.pallas_call(
        _ffn_kernel,
        grid=(b, l // tm, nf),
        in_specs=[
            pl.BlockSpec((1, 8, d), lambda bi, i, c: (bi, jnp.maximum(i * r8 - 1, 0), 0)),
            pl.BlockSpec((1, tm, d), lambda bi, i, c: (bi, i, 0)),
            pl.BlockSpec((1, 8, d), lambda bi, i, c: (bi, jnp.minimum((i + 1) * r8, n8 - 1), 0)),
            pl.BlockSpec((1, d), lambda bi, i, c: (0, 0)),
            pl.BlockSpec((d, tf), lambda bi, i, c: (0, c)),
            pl.BlockSpec((d, tf), lambda bi, i, c: (0, nf + c)),
            pl.BlockSpec((3, tf), lambda bi, i, c: (0, c)),
            pl.BlockSpec((3, tf), lambda bi, i, c: (0, nf + c)),
            pl.BlockSpec((1, tf), lambda bi, i, c: (0, c)),
            pl.BlockSpec((1, tf), lambda bi, i, c: (0, nf + c)),
            pl.BlockSpec((tf, d), lambda bi, i, c: (c, 0)),
            pl.BlockSpec((1, d), lambda bi, i, c: (0, 0)),
        ],
        out_specs=pl.BlockSpec((1, tm, d), lambda bi, i, c: (bi, i, 0)),
        out_shape=jax.ShapeDtypeStruct((b, l, d), F32),
        scratch_shapes=[pltpu.VMEM((tm + 16, d), BF16), pltpu.VMEM((tm, d), F32)],
        compiler_params=_params(("parallel", "parallel", "arbitrary")),
        name="conv_ffn",
    )(x, x, x, pre, w_in, w_in, conv_w, conv_w, conv_b, conv_b, w_out, post)


def _rot_half_cols(w, half):
    shp = w.shape
    w = w.reshape(shp[:-1] + (shp[-1] // (2 * half), 2, half))
    return jnp.concatenate([-w[..., 1:2, :], w[..., 0:1, :]], axis=-2).reshape(shp)


def _swap_half(g, half):
    shp = g.shape
    g = g.reshape(shp[:-1] + (shp[-1] // (2 * half), 2, half))
    return jnp.concatenate([g[..., 1:2, :], g[..., 0:1, :]], axis=-2).reshape(shp)


def _pack_w_in(w_in):
    sizes = (512, 512, 512, Q_LORA, KV_LORA, ROPE_B, D_INNER, D_INNER + 2 * N_GROUPS * D_STATE, H_C, H_C,
             H_D * DH_D, KV_D * DH_D, KV_D * DH_D, N_BRANCH * D_MODEL)
    offs = np.concatenate([[0], np.cumsum(sizes)])
    (a_q, a_k, a_v, b_cq, b_ckv, b_kr, c_z, c_xbc, c_dtf, c_dtb, d_q, d_k, d_v, g) = [
        w_in[:, offs[i]:offs[i + 1]] for i in range(len(sizes))]
    d = w_in.shape[0]
    z = lambda n: jnp.zeros((d, n), w_in.dtype)
    kr_blk = jnp.concatenate([z(NOPE_B), b_kr, z(LANE - NOPE_B - ROPE_B)], axis=1)
    krr_blk = jnp.concatenate([z(NOPE_B), _rot_half_cols(b_kr, ROPE_B // 2), z(LANE - NOPE_B - ROPE_B)], axis=1)
    dt_blk = jnp.concatenate([c_dtf, c_dtb, z(LANE - 2 * H_C)], axis=1)
    cols = [
        g,
        a_q * (DH_A ** -0.5 * LOG2E), a_k, a_v,
        c_z, c_xbc[:, :D_INNER],
        d_q, _rot_half_cols(d_q, DH_D // 4),
        b_cq, kr_blk, b_ckv, c_xbc[:, D_INNER:],
        krr_blk, dt_blk,
        d_k, _rot_half_cols(d_k, DH_D // 4), d_v,
        z((N_COLBLK - COL_DV - 1) * LANE),
    ]
    return jnp.concatenate(cols, axis=1).astype(BF16)


def _pad_heads(w, n_heads, per_head, take_from, take, place_at):
    d = w.shape[0]
    w = w.reshape(d, n_heads, per_head)[:, :, take_from:take_from + take]
    w = jnp.pad(w, ((0, 0), (0, 0), (place_at, LANE - place_at - take)))
    return w.reshape(d, n_heads * LANE)


def _rope_tables(l):
    half = ROPE_B // 2
    freqs = ROPE_THETA ** (-jnp.arange(half, dtype=F32) / half)
    pos = jnp.arange(l, dtype=F32)

    def cs(p):
        ang = p[:, None] * freqs[None, :]
        return jnp.cos(ang), jnp.sin(ang)

    c, s = cs(pos)
    ones = jnp.ones((l, NOPE_B), F32)
    zeros = jnp.zeros((l, LANE - NOPE_B - ROPE_B), F32)
    cos_b = jnp.concatenate([ones, c, c, zeros], axis=1)
    sin_b = jnp.concatenate([0 * ones, s, s, zeros], axis=1)
    cr, sr = cs(jnp.floor(pos / GRID_W))
    cc, sc = cs(pos - GRID_W * jnp.floor(pos / GRID_W))
    cos_d = jnp.concatenate([cr, cr, cc, cc] * 2, axis=1)
    sin_d = jnp.concatenate([sr, sr, sc, sc] * 2, axis=1)
    return cos_b, sin_b, cos_d, sin_d


def _layer(x, kv_mem, lp, layer_idx, tables):
    b, l, d = x.shape
    cos_b, sin_b, cos_d, sin_d = tables
    tm = min(512, l)
    t = b * l

    u = _norm_matmul(x.reshape(t, d), lp["norm_pre"][0], lp["w_in_packed"], BF16, min(1024, t), PROJ_TN)
    u = u.reshape(b, l, N_COLBLK * LANE)

    tq = min(1024, l)
    tk = tq // 2
    qt_a, k_a, vt_a = _prep_a(u, lp["alibi_slopes"], b, l, tm, tq, tk)
    o_a = _flash(qt_a, k_a, vt_a, lp["alibi"], tq=tq, tk=tk, alibi=True)
    qt_b, k_b, vt_b = _prep_b(u, b, l, tm, cos_b, sin_b, lp["mla_q_norm"].reshape(1, -1),
                              lp["mla_kv_norm"].reshape(1, -1), lp["wq_b"], lp["wqr_b"], lp["wk_b"], lp["wv_b"])
    o_b = _flash(qt_b, k_b, vt_b, lp["alibi"], tq=tq, tk=tk, alibi=False)
    qt_d, k_d, vt_d = _prep_d(u, b, l, tm, cos_d, sin_d, lp["gq_d"], lp["gqr_d"], lp["gk_d"], lp["gkr_d"])
    o_d = _flash(qt_d, k_d, vt_d, lp["alibi"], tq=tq, tk=tk, alibi=False)

    xs = _conv_silu(u, b, l, tm, COL_CXS // 4, D_INNER, lp["ssm_conv_w"][:, :D_INNER], lp["ssm_conv_b"][None, :D_INNER])
    bc = _conv_silu(u, b, l, tm, COL_CBC // 2, 2 * LANE, lp["ssm_conv_w"][:, D_INNER:], lp["ssm_conv_b"][None, D_INNER:])
    y_f = _ssd_scan(xs, bc, u, b, l, lp["alog_vec"], lp["dtb_vec"], reverse=False)
    y_c = _ssd_scan(xs, bc, u, b, l, lp["alog_vec"], lp["dtb_vec"], reverse=True,
                    final_args=(y_f, lp["dskip_x"], lp["ssm_norm"].reshape(1, -1)))

    lam_init = 0.8 - 0.6 * math.exp(-0.3 * layer_idx)
    x = _merge(x, o_a, o_b, y_c, o_d, u, lp["diff_lambda"], lp["diff_norm"].reshape(1, -1), lp["w_branch16"],
               lp["w_out16"], lp["norm_post"][0].reshape(1, -1), lam_init, min(256, l))

    x = _mem_attn(x, kv_mem, lp["norm_pre"][1].reshape(1, -1), lp["w_mem_q16"], lp["w_mem_o16"],
                  lp["norm_post"][1].reshape(1, -1), tm)

    x = _ffn(x, lp["norm_pre"][2].reshape(1, -1), lp["w_ffn_in16"], lp["ffn_conv_w"], lp["ffn_conv_b"].reshape(1, -1),
             lp["w_ffn_out16"], lp["norm_post"][2].reshape(1, -1), min(1024, l), 256)
    return x


def _prep_layer_params(p, layer_idx):
    lp = {name: arr[layer_idx] for name, arr in p.items()}
    lp["w_in_packed"] = _pack_w_in(lp["w_in"])
    qscale_b = (NOPE_B + ROPE_B) ** -0.5 * LOG2E
    wq = lp["w_mla_uq"] * qscale_b
    per_q = NOPE_B + ROPE_B
    lp["wq_b"] = _pad_heads(wq, H_B, per_q, 0, per_q, 0).astype(BF16)
    wq_rope = wq.reshape(Q_LORA, H_B, per_q)[:, :, NOPE_B:]
    lp["wqr_b"] = _pad_heads(_rot_half_cols(wq_rope, ROPE_B // 2).reshape(Q_LORA, H_B * ROPE_B), H_B, ROPE_B, 0, ROPE_B,
                             NOPE_B).astype(BF16)
    per_kv = NOPE_B + V_B
    lp["wk_b"] = _pad_heads(lp["w_mla_ukv"], H_B, per_kv, 0, NOPE_B, 0).astype(BF16)
    lp["wv_b"] = _pad_heads(lp["w_mla_ukv"], H_B, per_kv, NOPE_B, V_B, 0).astype(BF16)
    tile2 = lambda g: jnp.tile(g, 2).reshape(1, LANE)
    lp["gq_d"] = tile2(lp["gqa_q_norm"])
    lp["gqr_d"] = tile2(_swap_half(lp["gqa_q_norm"], DH_D // 4))
    lp["gk_d"] = tile2(lp["gqa_k_norm"])
    lp["gkr_d"] = tile2(_swap_half(lp["gqa_k_norm"], DH_D // 4))
    slopes = np.array([2.0 ** (-8.0 * (i + 1) / H_A) for i in range(H_A)], np.float32)
    lp["alibi_slopes"] = jnp.asarray(slopes)
    lp["alibi"] = jnp.asarray(np.repeat(slopes * np.float32(LOG2E), 2))
    pad_vec = lambda v: jnp.pad(v.reshape(-1), (0, LANE - 2 * H_C)).reshape(1, LANE)
    lp["alog_vec"] = pad_vec(lp["ssm_a_log"])
    lp["dtb_vec"] = pad_vec(lp["ssm_dt_bias"])
    lp["dskip_x"] = jnp.repeat(lp["ssm_d"], SSM_HEAD).reshape(1, D_INNER)
    lp["w_branch16"] = lp["w_branch"].astype(BF16)
    lp["w_out16"] = lp["w_out"].astype(BF16)
    lp["w_mem_q16"] = (lp["w_mem_q"] * (DH_MEM ** -0.5 * LOG2E)).astype(BF16)
    lp["w_mem_kv16"] = lp["w_mem_kv"].astype(BF16)
    lp["w_mem_o16"] = lp["w_mem_o"].astype(BF16)
    lp["w_ffn_in16"] = lp["w_ffn_in"].astype(BF16)
    lp["w_ffn_out16"] = lp["w_ffn_out"].astype(BF16)
    return lp


def _trunk(x, mem, layer_params):
    b, l, d = x.shape
    m = mem.shape[1]
    tables = _rope_tables(l)
    for layer_idx, lp in enumerate(layer_params):
        kv = _norm_matmul(mem.reshape(b * m, d), lp["mem_norm"], lp["w_mem_kv16"], BF16, min(256, b * m), 512)
        x = _layer(x, kv.reshape(b, m, 2 * H_MEM * DH_MEM), lp, layer_idx, tables)
    return x


def kernel(x_prompt, x_sample, mem_prompt, mem_sample, w_in, w_branch, w_out, diff_lambda, diff_norm, mla_q_norm,
           mla_kv_norm, w_mla_uq, w_mla_ukv, ssm_conv_w, ssm_conv_b, ssm_a_log, ssm_dt_bias, ssm_d, ssm_norm,
           gqa_q_norm, gqa_k_norm, mem_norm, w_mem_q, w_mem_kv, w_mem_o, w_ffn_in, ffn_conv_w, ffn_conv_b, w_ffn_out,
           norm_pre, norm_post):
    p = {
        "w_in": w_in, "w_branch": w_branch, "w_out": w_out, "diff_lambda": diff_lambda, "diff_norm": diff_norm,
        "mla_q_norm": mla_q_norm, "mla_kv_norm": mla_kv_norm, "w_mla_uq": w_mla_uq, "w_mla_ukv": w_mla_ukv,
        "ssm_conv_w": ssm_conv_w, "ssm_conv_b": ssm_conv_b, "ssm_a_log": ssm_a_log, "ssm_dt_bias": ssm_dt_bias,
        "ssm_d": ssm_d, "ssm_norm": ssm_norm, "gqa_q_norm": gqa_q_norm, "gqa_k_norm": gqa_k_norm,
        "mem_norm": mem_norm, "w_mem_q": w_mem_q, "w_mem_kv": w_mem_kv, "w_mem_o": w_mem_o,
        "w_ffn_in": w_ffn_in, "ffn_conv_w": ffn_conv_w, "ffn_conv_b": ffn_conv_b, "w_ffn_out": w_ffn_out,
        "norm_pre": norm_pre, "norm_post": norm_post,
    }
    layer_params = [_prep_layer_params(p, i) for i in range(DEPTH)]
    y_prompt = _trunk(x_prompt, mem_prompt, layer_params)
    y_sample = _trunk(x_sample, mem_sample, layer_params)
    return (y_prompt, y_sample)
```

```python
import functools
import math

import jax
import jax.numpy as jnp
import numpy as np
from jax import lax
from jax.experimental import pallas as pl
from jax.experimental.pallas import tpu as pltpu

F32 = jnp.float32
BF16 = jnp.bfloat16

D_MODEL = 1024
DEPTH = 2
GRID_W = 64
ROPE_THETA = 10000.0
EPS = 1e-6
N_BRANCH = 4
BRANCH_W = 512
H_A = 4
DH_A = 64
H_B = 4
Q_LORA = 384
KV_LORA = 256
NOPE_B = 64
ROPE_B = 32
V_B = 128
D_INNER = 512
SSM_HEAD = 64
H_C = D_INNER // SSM_HEAD
N_GROUPS = 2
D_STATE = 64
CHUNK = 128
H_D = 8
KV_D = 2
DH_D = 64
H_MEM = 4
DH_MEM = 128
D_FF = 2816

LANE = 128
LOG2E = 1.4426950408889634
NEG_BIG = -1e30
FLASH_CB = 512
ONES_ROWS = 16
VMEM_LIMIT = 56 * 1024 * 1024

COL_G = 0
COL_AQ = 32
COL_AK = 36
COL_AV = 40
COL_CZ = 44
COL_CXS = 48
COL_DQ = 52
COL_DQR = 56
COL_BCQ = 60
COL_BKR = 63
COL_BCKV = 64
COL_CBC = 66
COL_BKRR = 68
COL_CDT = 69
COL_DK = 70
COL_DKR = 71
COL_DV = 72
N_COLBLK = 76
PROJ_TN = (N_COLBLK // 4) * LANE


def _params(sem, flags=None):
    return pltpu.CompilerParams(dimension_semantics=sem, vmem_limit_bytes=VMEM_LIMIT, flags=flags)


def _rms(x, g):
    return x * lax.rsqrt(jnp.mean(x * x, axis=-1, keepdims=True) + EPS) * g


def _bf16_parts(x):
    p1 = x.astype(BF16)
    r1 = x - p1.astype(F32)
    p2 = r1.astype(BF16)
    p3 = (r1 - p2.astype(F32)).astype(BF16)
    return p1, p2, p3


def _dot_select(x, sel, sel_first=False):
    sel = sel.astype(BF16)
    parts = _bf16_parts(x)
    outs = [jnp.dot(sel, p, preferred_element_type=F32) if sel_first else jnp.dot(p, sel, preferred_element_type=F32)
            for p in parts]
    return outs[0] + (outs[1] + outs[2])


def _norm_matmul_kernel(x_ref, g_ref, w_ref, o_ref, xn_ref):
    @pl.when(pl.program_id(1) == 0)
    def _():
        xn_ref[...] = _rms(x_ref[...], g_ref[...]).astype(BF16)

    o_ref[...] = jnp.dot(xn_ref[...], w_ref[...], preferred_element_type=F32).astype(o_ref.dtype)


def _norm_matmul(x, g, w, out_dtype, tm, tn):
    t, d = x.shape
    n = w.shape[1]
    return pl.pallas_call(
        _norm_matmul_kernel,
        grid=(t // tm, n // tn),
        in_specs=[
            pl.BlockSpec((tm, d), lambda i, j: (i, 0)),
            pl.BlockSpec((1, d), lambda i, j: (0, 0)),
            pl.BlockSpec((d, tn), lambda i, j: (0, j)),
        ],
        out_specs=pl.BlockSpec((tm, tn), lambda i, j: (i, j)),
        out_shape=jax.ShapeDtypeStruct((t, n), out_dtype),
        scratch_shapes=[pltpu.VMEM((tm, d), BF16)],
        compiler_params=_params(("parallel", "arbitrary")),
        name="norm_matmul",
    )(x, g.reshape(1, d), w)


def _ones_rows(tm):
    row = lax.broadcasted_iota(jnp.int32, (ONES_ROWS, tm), 0)
    return jnp.where(row == 0, 1.0, 0.0).astype(BF16)


def _bf16_terms(x, n):
    terms = []
    for _ in range(n):
        t = float(np.asarray(x, np.float32).astype(BF16).astype(np.float32))
        terms.append(t)
        x = x - t
    return terms


LOG2E_TERMS = _bf16_terms(LOG2E, 3)
POS_SPLIT = 32


def _alibi_features(idx, along, slope, key_side):
    slot = lax.broadcasted_iota(jnp.int32, idx.shape, along)
    lo_i = jnp.bitwise_and(idx, POS_SPLIT - 1)
    lo = lo_i.astype(F32)
    hi = (idx - lo_i).astype(F32)
    t = jnp.where(slot >= 9, slot - 9, jnp.where(slot >= 6, slot - 6, jnp.where(slot >= 3, slot - 3, slot)))
    c = slope * jnp.where(t == 0, LOG2E_TERMS[0], jnp.where(t == 1, LOG2E_TERMS[1], LOG2E_TERMS[2]))
    if key_side:
        f = jnp.where(slot < 3, lo, jnp.where(slot < 6, hi, jnp.where(slot < 12, c, 0.0)))
    else:
        f = jnp.where(slot < 6, c, jnp.where(slot < 9, -lo, jnp.where(slot < 12, -hi, 0.0)))
    return f


def _prep_a_kernel(slope_ref, q_ref, k_ref, v_ref, qt_ref, ko_ref, vt_ref, *, tq, tk):
    q = q_ref[0].astype(F32)
    tm = q.shape[0]
    row0 = pl.program_id(1) * tm
    slope = slope_ref[pl.program_id(2)]
    lane = lax.broadcasted_iota(jnp.int32, q.shape, 1)
    q1 = jnp.where(lane < DH_A, q, 0.0)
    j = jnp.bitwise_and(row0 + lax.broadcasted_iota(jnp.int32, (LANE, tm), 1), tq - 1)
    fq = _alibi_features(j, 0, slope, key_side=False).astype(BF16)
    qt_ref[0, 0, 0:LANE, :] = q1.T.astype(BF16)
    qt_ref[0, 0, LANE:, :] = fq
    qt_ref[0, 1, 0:LANE, :] = (q - q1).T.astype(BF16)
    qt_ref[0, 1, LANE:, :] = fq
    i = jnp.bitwise_and(row0 + lax.broadcasted_iota(jnp.int32, (tm, LANE), 0), tk - 1)
    ko_ref[0, 0, :, 0:LANE] = k_ref[0].astype(BF16)
    ko_ref[0, 0, :, LANE:] = _alibi_features(i, 1, slope, key_side=True).astype(BF16)
    vt_ref[0, 0, 0:2 * DH_A, :] = v_ref[0].astype(F32).T.astype(BF16)
    vt_ref[0, 0, 2 * DH_A:, :] = _ones_rows(tm)


def _prep_a(u, slopes, b, l, tm, tq, tk):
    dv = 2 * DH_A + ONES_ROWS
    return pl.pallas_call(
        functools.partial(_prep_a_kernel, tq=tq, tk=tk),
        grid=(b, l // tm, H_A),
        in_specs=[
            pl.BlockSpec(memory_space=pltpu.SMEM),
            pl.BlockSpec((1, tm, LANE), lambda bi, i, h: (bi, i, COL_AQ + h)),
            pl.BlockSpec((1, tm, LANE), lambda bi, i, h: (bi, i, COL_AK + h)),
            pl.BlockSpec((1, tm, LANE), lambda bi, i, h: (bi, i, COL_AV + h)),
        ],
        out_specs=[
            pl.BlockSpec((1, 2, 2 * LANE, tm), lambda bi, i, h: (bi, h, 0, i)),
            pl.BlockSpec((1, 1, tm, 2 * LANE), lambda bi, i, h: (bi, h, i, 0)),
            pl.BlockSpec((1, 1, dv, tm), lambda bi, i, h: (bi, h, 0, i)),
        ],
        out_shape=[
            jax.ShapeDtypeStruct((b, 2 * H_A, 2 * LANE, l), BF16),
            jax.ShapeDtypeStruct((b, H_A, l, 2 * LANE), BF16),
            jax.ShapeDtypeStruct((b, H_A, dv, l), BF16),
        ],
        compiler_params=_params(("parallel", "parallel", "parallel")),
        name="prep_diff",
    )(slopes, u, u, u)


def _prep_b_kernel(cq_ref, ckv_ref, kr_ref, krr_ref, cos_ref, sin_ref, gq_ref, gkv_ref,
                   wq_ref, wqr_ref, wk_ref, wv_ref, qt_ref, ko_ref, vt_ref):
    tm = cq_ref.shape[1]
    cos = cos_ref[...]
    sin = sin_ref[...]
    cqn = _rms(cq_ref[0].astype(F32), gq_ref[...]).astype(BF16)
    ckn = _rms(ckv_ref[0].astype(F32), gkv_ref[...]).astype(BF16)
    qa = jnp.dot(cqn, wq_ref[...], preferred_element_type=F32)
    qb = jnp.dot(cqn, wqr_ref[...], preferred_element_type=F32)
    kn = jnp.dot(ckn, wk_ref[...], preferred_element_type=F32)
    vv = jnp.dot(ckn, wv_ref[...], preferred_element_type=F32)
    krope = kr_ref[0].astype(F32) * cos + krr_ref[0].astype(F32) * sin
    ones = _ones_rows(tm)
    for h in range(H_B):
        sl = slice(h * LANE, (h + 1) * LANE)
        qh = qa[:, sl] * cos + qb[:, sl] * sin
        qt_ref[0, h] = qh.T.astype(BF16)
        ko_ref[0, h] = (kn[:, sl] + krope).astype(BF16)
        vt_ref[0, h, 0:V_B, :] = vv[:, sl].T.astype(BF16)
        vt_ref[0, h, V_B:, :] = ones


def _prep_b(u, b, l, tm, cos, sin, gq, gkv, wq, wqr, wk, wv):
    dv = V_B + ONES_ROWS
    const = lambda shape: pl.BlockSpec(shape, lambda bi, i: (0,) * len(shape))
    return pl.pallas_call(
        _prep_b_kernel,
        grid=(b, l // tm),
        in_specs=[
            pl.BlockSpec((1, tm, Q_LORA), lambda bi, i: (bi, i, COL_BCQ // 3)),
            pl.BlockSpec((1, tm, KV_LORA), lambda bi, i: (bi, i, COL_BCKV // 2)),
            pl.BlockSpec((1, tm, LANE), lambda bi, i: (bi, i, COL_BKR)),
            pl.BlockSpec((1, tm, LANE), lambda bi, i: (bi, i, COL_BKRR)),
            pl.BlockSpec((tm, LANE), lambda bi, i: (i, 0)),
            pl.BlockSpec((tm, LANE), lambda bi, i: (i, 0)),
            const((1, Q_LORA)), const((1, KV_LORA)),
            const((Q_LORA, H_B * LANE)), const((Q_LORA, H_B * LANE)),
            const((KV_LORA, H_B * LANE)), const((KV_LORA, H_B * LANE)),
        ],
        out_specs=[
            pl.BlockSpec((1, H_B, LANE, tm), lambda bi, i: (bi, 0, 0, i)),
            pl.BlockSpec((1, H_B, tm, LANE), lambda bi, i: (bi, 0, i, 0)),
            pl.BlockSpec((1, H_B, dv, tm), lambda bi, i: (bi, 0, 0, i)),
        ],
        out_shape=[
            jax.ShapeDtypeStruct((b, H_B, LANE, l), BF16),
            jax.ShapeDtypeStruct((b, H_B, l, LANE), BF16),
            jax.ShapeDtypeStruct((b, H_B, dv, l), BF16),
        ],
        compiler_params=_params(("parallel", "parallel")),
        name="prep_mla",
    )(u, u, u, u, cos, sin, gq, gkv, wq, wqr, wk, wv)


def _prep_d_kernel(q_ref, qr_ref, k_ref, kr_ref, v_ref, cos_ref, sin_ref, gq_ref, gqr_ref, gk_ref, gkr_ref,
                   qt_ref, ko_ref, vt_ref, *, qscale):
    tm = q_ref.shape[1]
    cos = cos_ref[...]
    sin = sin_ref[...]
    r_i = lax.broadcasted_iota(jnp.int32, (LANE, LANE), 0) // DH_D
    c_i = lax.broadcasted_iota(jnp.int32, (LANE, LANE), 1) // DH_D
    blockdiag = jnp.where(r_i == c_i, 1.0, 0.0).astype(F32)

    def norm_rope(x, xr, g, gr):
        ss = _dot_select(x * x, blockdiag)
        r = lax.rsqrt(ss * (1.0 / DH_D) + EPS)
        return r * (x * g * cos + xr * gr * sin)

    q = q_ref[0].astype(F32)
    qr = qr_ref[0].astype(F32)
    zeros = jnp.zeros((DH_D, tm), F32)
    for j in range(H_D // 2):
        sl = slice(j * LANE, (j + 1) * LANE)
        y = norm_rope(q[:, sl], qr[:, sl], gq_ref[...], gqr_ref[...]) * qscale
        yt = y.T
        grp = (2 * j) // (H_D // KV_D)
        for e in range(2):
            piece = yt[e * DH_D:(e + 1) * DH_D]
            full = jnp.concatenate([piece, zeros], axis=0) if grp == 0 else jnp.concatenate([zeros, piece], axis=0)
            qt_ref[0, 2 * j + e] = full.astype(BF16)
    ko_ref[0, 0] = norm_rope(k_ref[0].astype(F32), kr_ref[0].astype(F32), gk_ref[...], gkr_ref[...]).astype(BF16)
    vt = v_ref[0].astype(F32).T
    ones = _ones_rows(tm)
    for grp in range(KV_D):
        vt_ref[0, grp, 0:DH_D, :] = vt[grp * DH_D:(grp + 1) * DH_D].astype(BF16)
        vt_ref[0, grp, DH_D:, :] = ones


def _prep_d(u, b, l, tm, cos, sin, gq, gqr, gk, gkr):
    dv = DH_D + ONES_ROWS
    const = lambda shape: pl.BlockSpec(shape, lambda bi, i: (0,) * len(shape))
    return pl.pallas_call(
        functools.partial(_prep_d_kernel, qscale=DH_D ** -0.5 * LOG2E),
        grid=(b, l // tm),
        in_specs=[
            pl.BlockSpec((1, tm, 4 * LANE), lambda bi, i: (bi, i, COL_DQ // 4)),
            pl.BlockSpec((1, tm, 4 * LANE), lambda bi, i: (bi, i, COL_DQR // 4)),
            pl.BlockSpec((1, tm, LANE), lambda bi, i: (bi, i, COL_DK)),
            pl.BlockSpec((1, tm, LANE), lambda bi, i: (bi, i, COL_DKR)),
            pl.BlockSpec((1, tm, LANE), lambda bi, i: (bi, i, COL_DV)),
            pl.BlockSpec((tm, LANE), lambda bi, i: (i, 0)),
            pl.BlockSpec((tm, LANE), lambda bi, i: (i, 0)),
            const((1, LANE)), const((1, LANE)), const((1, LANE)), const((1, LANE)),
        ],
        out_specs=[
            pl.BlockSpec((1, H_D, LANE, tm), lambda bi, i: (bi, 0, 0, i)),
            pl.BlockSpec((1, 1, tm, LANE), lambda bi, i: (bi, 0, i, 0)),
            pl.BlockSpec((1, KV_D, dv, tm), lambda bi, i: (bi, 0, 0, i)),
        ],
        out_shape=[
            jax.ShapeDtypeStruct((b, H_D, LANE, l), BF16),
            jax.ShapeDtypeStruct((b, 1, l, LANE), BF16),
            jax.ShapeDtypeStruct((b, KV_D, dv, l), BF16),
        ],
        compiler_params=_params(("parallel", "parallel")),
        name="prep_gqa",
    )(u, u, u, u, u, cos, sin, gq, gqr, gk, gkr)


def _flash_kernel(slope_ref, qt_ref, k_ref, vt_ref, o_ref, m_ref, acc_ref, sa_ref, sb_ref, pa_ref, pb_ref,
                  wa_ref, wb_ref, ala_ref, alb_ref, *, tk, cb, stages, dv, alibi):
    tq = qt_ref.shape[3]
    l = k_ref.shape[2]
    n_pairs = l // (2 * tk)
    m_ref[...] = jnp.full(m_ref.shape, NEG_BIG, F32)
    acc_ref[...] = jnp.zeros(acc_ref.shape, F32)
    if stages == 3:
        wb_ref[...] = jnp.zeros(wb_ref.shape, BF16)
        alb_ref[...] = jnp.ones(alb_ref.shape, F32)
    q0 = pl.program_id(2) * tq
    slope = slope_ref[pl.program_id(1)] if alibi else None

    n_cb = tq // cb

    def scores(k0, s_ref, p_ref, diag, c):
        cs = slice(c * cb, (c + 1) * cb)
        if not alibi:
            s = jnp.dot(k_ref[0, 0, pl.ds(k0, tk), :], qt_ref[0, 0, :, cs], preferred_element_type=F32)
        elif diag:
            rel = lax.broadcasted_iota(jnp.int32, (tk, cb), 0) - lax.broadcasted_iota(jnp.int32, (tk, cb), 1)
            s = jnp.dot(k_ref[0, 0, pl.ds(k0, tk), 0:LANE], qt_ref[0, 0, 0:LANE, cs], preferred_element_type=F32)
            s = s - slope * jnp.abs(rel + (k0 - q0 - c * cb)).astype(F32)
        else:
            sign = jnp.where(k0 < q0, 1.0, -1.0).astype(BF16)
            kt = jnp.concatenate([k_ref[0, 0, pl.ds(k0, tk), 0:LANE], k_ref[0, 0, pl.ds(k0, tk), LANE:] * sign], axis=1)
            s = jnp.dot(kt, qt_ref[0, 0, :, cs], preferred_element_type=F32)
        s_ref[:, cs] = s
        p_ref[:, cs] = jnp.max(s.reshape(tk // 8, 8, cb), axis=0)

    def weights(s_ref, p_ref, w_ref, al_ref, k0, diag, c):
        cs = slice(c * cb, (c + 1) * cb)
        off = slope * jnp.abs(k0 - q0).astype(F32) if (alibi and not diag) else 0.0
        m_old = m_ref[:, cs]
        m_new = jnp.maximum(m_old, jnp.max(p_ref[:, cs], axis=0, keepdims=True) - off)
        alpha = jnp.exp2(m_old - m_new)
        w = jnp.exp2((s_ref[:, cs] - (m_new + off)).astype(BF16))
        m_ref[:, cs] = m_new
        if stages == 3:
            al_ref[:, cs] = alpha
            w_ref[:, cs] = w
        else:
            pv = jnp.dot(vt_ref[0, 0, :, pl.ds(k0, tk)], w, preferred_element_type=F32)
            acc_ref[:, cs] = alpha * acc_ref[:, cs] + pv

    def values(w_ref, al_ref, k0, c):
        cs = slice(c * cb, (c + 1) * cb)
        pv = jnp.dot(vt_ref[0, 0, :, pl.ds(k0, tk)], w_ref[:, cs], preferred_element_type=F32)
        acc_ref[:, cs] = al_ref[:, cs] * acc_ref[:, cs] + pv

    def pair(j, diag_cur, diag_next):
        k0 = pl.multiple_of(j * (2 * tk), 2 * tk)
        k1 = pl.multiple_of(k0 + tk, tk)
        k2 = pl.multiple_of(jnp.minimum(k0 + 2 * tk, l - tk), tk)
        kp = pl.multiple_of(jnp.maximum(k0 - tk, 0), tk)
        for c in range(n_cb):
            scores(k1, sb_ref, pb_ref, diag_cur, c)
            weights(sa_ref, pa_ref, wa_ref, ala_ref, k0, diag_cur, c)
            if stages == 3:
                values(wb_ref, alb_ref, kp, c)
        for c in range(n_cb):
            scores(k2, sa_ref, pa_ref, diag_next, c)
            weights(sb_ref, pb_ref, wb_ref, alb_ref, k1, diag_cur, c)
            if stages == 3:
                values(wa_ref, ala_ref, k0, c)

    def first_scores(diag):
        for c in range(n_cb):
            scores(0, sa_ref, pa_ref, diag, c)

    if alibi:
        jd = pl.program_id(2)
        lax.cond(jd == 0, lambda: first_scores(True), lambda: first_scores(False))

        def body(j, carry):
            which = jnp.where(j == jd, 2, jnp.where(j + 1 == jd, 1, 0))
            lax.switch(which, [lambda: pair(j, False, False), lambda: pair(j, False, True),
                               lambda: pair(j, True, False)])
            return carry
    else:
        first_scores(False)

        def body(j, carry):
            pair(j, False, False)
            return carry

    lax.fori_loop(0, n_pairs, body, 0)
    if stages == 3:
        for c in range(n_cb):
            values(wb_ref, alb_ref, l - tk, c)
    acc = acc_ref[...]
    o_ref[0, 0] = acc[0:dv] / acc[dv:dv + 1]


def _flash(qt, k, vt, slopes, *, tq, tk, alibi, cb=None, stages=2, flags=None):
    b, hq, dk, l = qt.shape
    hk = k.shape[1]
    hv = vt.shape[1]
    dvp = vt.shape[2]
    dv = dvp - ONES_ROWS
    assert l % (2 * tk) == 0 and l % tq == 0 and (not alibi or tq == 2 * tk)
    wshape = (tk, tq) if stages == 3 else (BF16_ROWS, LANE)
    return pl.pallas_call(
        functools.partial(_flash_kernel, tk=tk, cb=min(cb or tq, tq), stages=stages, dv=dv, alibi=alibi),
        grid=(b, hq, l // tq),
        in_specs=[
            pl.BlockSpec(memory_space=pltpu.SMEM),
            pl.BlockSpec((1, 1, dk, tq), lambda bi, h, i: (bi, h, 0, i)),
            pl.BlockSpec((1, 1, l, dk), lambda bi, h, i: (bi, h // (hq // hk), 0, 0)),
            pl.BlockSpec((1, 1, dvp, l), lambda bi, h, i: (bi, h // (hq // hv), 0, 0)),
        ],
        out_specs=pl.BlockSpec((1, 1, dv, tq), lambda bi, h, i: (bi, h, 0, i)),
        out_shape=jax.ShapeDtypeStruct((b, hq, dv, l), F32),
        scratch_shapes=[pltpu.VMEM((1, tq), F32), pltpu.VMEM((dvp, tq), F32),
                        pltpu.VMEM((tk, tq), F32), pltpu.VMEM((tk, tq), F32),
                        pltpu.VMEM((8, tq), F32), pltpu.VMEM((8, tq), F32),
                        pltpu.VMEM(wshape, BF16), pltpu.VMEM(wshape, BF16),
                        pltpu.VMEM((1, tq), F32), pltpu.VMEM((1, tq), F32)],
        compiler_params=_params(("parallel", "parallel", "parallel"), flags),
        name="flash_" + ("alibi" if alibi else "plain") + f"_{hq}_{dv}_q{tq}k{tk}c{cb or tq}s{stages}",
    )(slopes, qt, k, vt)


def _conv3_rows(xe, w_ref, b_ref, tm):
    n = xe.shape[0]
    halo = (n - tm) // 2
    up = pltpu.roll(xe, 1, 0)
    dn = pltpu.roll(xe, n - 1, 0)
    y = up * w_ref[0:1, :] + xe * w_ref[1:2, :] + dn * w_ref[2:3, :] + b_ref[...]
    return y[halo:halo + tm]


def _halo_rows(prev_ref, main_ref, next_ref, i, n_i):
    prev = jnp.where(i == 0, 0.0, prev_ref[0].astype(F32))
    nxt = jnp.where(i == n_i - 1, 0.0, next_ref[0].astype(F32))
    return jnp.concatenate([prev, main_ref[0].astype(F32), nxt], axis=0)


def _conv_silu_kernel(p_ref, x_ref, n_ref, w_ref, b_ref, o_ref):
    tm = x_ref.shape[1]
    xe = _halo_rows(p_ref, x_ref, n_ref, pl.program_id(1), pl.num_programs(1))
    y = _conv3_rows(xe, w_ref, b_ref, tm)
    o_ref[0] = y * (1.0 / (1.0 + jnp.exp(-y)))


BF16_ROWS = 16


def _conv_silu(u, b, l, tm, col_blk, width, w, bias):
    r8 = tm // BF16_ROWS
    n8 = l // BF16_ROWS
    return pl.pallas_call(
        _conv_silu_kernel,
        grid=(b, l // tm),
        in_specs=[
            pl.BlockSpec((1, BF16_ROWS, width), lambda bi, i: (bi, jnp.maximum(i * r8 - 1, 0), col_blk)),
            pl.BlockSpec((1, tm, width), lambda bi, i: (bi, i, col_blk)),
            pl.BlockSpec((1, BF16_ROWS, width), lambda bi, i: (bi, jnp.minimum((i + 1) * r8, n8 - 1), col_blk)),
            pl.BlockSpec((3, width), lambda bi, i: (0, 0)),
            pl.BlockSpec((1, width), lambda bi, i: (0, 0)),
        ],
        out_specs=pl.BlockSpec((1, tm, width), lambda bi, i: (bi, i, 0)),
        out_shape=jax.ShapeDtypeStruct((b, l, width), F32),
        compiler_params=_params(("parallel", "parallel")),
        name=f"conv_silu_{width}",
    )(u, u, u, w, bias)


def _ssd_kernel(*refs, reverse, final):
    if final:
        (xs_ref, bc_b_ref, bc_c_ref, dt_ref, alog_ref, dtb_ref, yprev_ref, z_ref, dskip_ref, norm_ref,
         y_ref, state_ref) = refs
    else:
        xs_ref, bc_b_ref, bc_c_ref, dt_ref, alog_ref, dtb_ref, y_ref, state_ref = refs
    q = CHUNK

    @pl.when(pl.program_id(1) == 0)
    def _():
        state_ref[...] = jnp.zeros(state_ref.shape, F32)

    row = lax.broadcasted_iota(jnp.int32, (q, q), 0)
    col = lax.broadcasted_iota(jnp.int32, (q, q), 1)
    mask = (col >= row) if reverse else (col <= row)
    tri = jnp.where(mask, 1.0, 0.0).astype(F32)
    head_off = H_C if reverse else 0

    x = dt_ref[0].astype(F32) + dtb_ref[...]
    dt = jnp.maximum(x, 0.0) + jnp.log(1.0 + jnp.exp(-jnp.abs(x)))
    a = dt * (-jnp.exp(alog_ref[...]))
    cs = _dot_select(a, tri, sel_first=True)
    e_r = lax.broadcasted_iota(jnp.int32, (q, D_INNER), 0)
    e_c = lax.broadcasted_iota(jnp.int32, (q, D_INNER), 1) // SSM_HEAD + head_off
    expand = jnp.where(e_r == e_c, 1.0, 0.0).astype(F32)
    dt_x = _dot_select(dt, expand)
    cs_x = _dot_select(cs, expand)
    tot_x = cs_x[0:1] if reverse else cs_x[q - 1:q]

    xs = xs_ref[0]
    xdt = xs * dt_x
    w_in = (xdt * jnp.exp(tot_x - cs_x)).astype(BF16)
    bt = bc_b_ref[0].T
    bt16 = bt.astype(BF16)
    cm = bc_c_ref[0]
    cs_t = cs.T
    lane = lax.broadcasted_iota(jnp.int32, (q, LANE), 1)
    rowi = lax.broadcasted_iota(jnp.int32, (q, LANE), 0)

    outs = []
    for pair in range(H_C // 2):
        grp = (2 * pair) // (H_C // N_GROUPS)
        in_grp = (lane // D_STATE) == grp
        c_g = jnp.where(in_grp, cm, 0.0).astype(BF16)
        cb = jnp.dot(c_g, bt16, preferred_element_type=F32)
        psl = slice(pair * LANE, (pair + 1) * LANE)
        x_p = xdt[:, psl].astype(BF16)
        ys = []
        for e in range(2):
            hl = head_off + 2 * pair + e
            seg = cs[:, hl:hl + 1] - cs_t[hl:hl + 1, :]
            dec = jnp.exp(jnp.where(mask, seg, -jnp.inf))
            ys.append(jnp.dot((cb * dec).astype(BF16), x_p, preferred_element_type=F32))
        y_diag = jnp.where(lane < SSM_HEAD, ys[0], ys[1])
        st = state_ref[pair]
        y_off = jnp.dot(c_g, st.astype(BF16), preferred_element_type=F32) * jnp.exp(cs_x[:, psl])
        outs.append(y_diag + y_off)
        upd = jnp.dot(bt16, w_in[:, psl], preferred_element_type=F32)
        keep = (rowi // D_STATE) == grp
        state_ref[pair] = jnp.where(keep, st * jnp.exp(tot_x[:, psl]) + upd, 0.0)
    y = jnp.concatenate(outs, axis=1)
    if final:
        y = y + yprev_ref[0] + dskip_ref[...] * xs
        z = z_ref[0].astype(F32)
        y = y * (z * (1.0 / (1.0 + jnp.exp(-z))))
        y = _rms(y, norm_ref[...])
    y_ref[0] = y


def _ssd_scan(xs, bc, u, b, l, alog, dtb, reverse, final_args=None):
    nc = l // CHUNK
    cidx = (lambda c: nc - 1 - c) if reverse else (lambda c: c)
    in_specs = [
        pl.BlockSpec((1, CHUNK, D_INNER), lambda bi, c: (bi, cidx(c), 0)),
        pl.BlockSpec((1, CHUNK, LANE), lambda bi, c: (bi, cidx(c), 0)),
        pl.BlockSpec((1, CHUNK, LANE), lambda bi, c: (bi, cidx(c), 1)),
        pl.BlockSpec((1, CHUNK, LANE), lambda bi, c: (bi, cidx(c), COL_CDT)),
        pl.BlockSpec((1, LANE), lambda bi, c: (0, 0)),
        pl.BlockSpec((1, LANE), lambda bi, c: (0, 0)),
    ]
    args = [xs, bc, bc, u, alog, dtb]
    final = final_args is not None
    if final:
        yprev, dskip, norm = final_args
        in_specs += [
            pl.BlockSpec((1, CHUNK, D_INNER), lambda bi, c: (bi, cidx(c), 0)),
            pl.BlockSpec((1, CHUNK, D_INNER), lambda bi, c: (bi, cidx(c), COL_CZ // 4)),
            pl.BlockSpec((1, D_INNER), lambda bi, c: (0, 0)),
            pl.BlockSpec((1, D_INNER), lambda bi, c: (0, 0)),
        ]
        args += [yprev, u, dskip, norm]
    return pl.pallas_call(
        functools.partial(_ssd_kernel, reverse=reverse, final=final),
        grid=(b, nc),
        in_specs=in_specs,
        out_specs=pl.BlockSpec((1, CHUNK, D_INNER), lambda bi, c: (bi, cidx(c), 0)),
        out_shape=jax.ShapeDtypeStruct((b, l, D_INNER), F32),
        scratch_shapes=[pltpu.VMEM((H_C // 2, LANE, LANE), F32)],
        compiler_params=_params(("parallel", "arbitrary")),
        name="ssd_bwd" if reverse else "ssd_fwd",
    )(*args)


def _merge_kernel(x_ref, oa_ref, ob_ref, yc_ref, od_ref, g_ref, lam_ref, subln_ref, wb_ref, wo_ref, post_ref,
                  o_ref, *, lam_init):
    lp = lam_ref[...]
    lam = (jnp.exp(jnp.sum(lp[0:1] * lp[1:2], axis=-1, keepdims=True))
           - jnp.exp(jnp.sum(lp[2:3] * lp[3:4], axis=-1, keepdims=True)) + lam_init)
    ya = []
    for h in range(H_A):
        o = (oa_ref[0, 2 * h] - lam * oa_ref[0, 2 * h + 1]).T
        ya.append(_rms(o, subln_ref[...]) * (1.0 - lam_init))
    ya = jnp.concatenate(ya, axis=1).astype(BF16)
    yb = jnp.concatenate([ob_ref[0, h].T for h in range(H_B)], axis=1).astype(BF16)
    yc = yc_ref[0].astype(BF16)
    yd = jnp.concatenate([jnp.concatenate([od_ref[0, 2 * j], od_ref[0, 2 * j + 1]], axis=0).T
                          for j in range(H_D // 2)], axis=1).astype(BF16)
    merged = None
    for n, y in enumerate((ya, yb, yc, yd)):
        proj = jnp.dot(y, wb_ref[n], preferred_element_type=F32)
        gn = g_ref[0, :, n * D_MODEL:(n + 1) * D_MODEL].astype(F32)
        term = proj * (1.0 / (1.0 + jnp.exp(-gn)))
        merged = term if merged is None else merged + term
    out = jnp.dot(merged.astype(BF16), wo_ref[...], preferred_element_type=F32)
    o_ref[0] = x_ref[0] + _rms(out, post_ref[...])


def _merge(x, oa, ob, yc, od, u, lam_p, subln, wb, wo, post, lam_init, tm):
    b, l, d = x.shape
    const = lambda shape: pl.BlockSpec(shape, lambda bi, i: (0,) * len(shape))
    return pl.pallas_call(
        functools.partial(_merge_kernel, lam_init=lam_init),
        grid=(b, l // tm),
        in_specs=[
            pl.BlockSpec((1, tm, d), lambda bi, i: (bi, i, 0)),
            pl.BlockSpec((1, 2 * H_A, LANE, tm), lambda bi, i: (bi, 0, 0, i)),
            pl.BlockSpec((1, H_B, V_B, tm), lambda bi, i: (bi, 0, 0, i)),
            pl.BlockSpec((1, tm, D_INNER), lambda bi, i: (bi, i, 0)),
            pl.BlockSpec((1, H_D, DH_D, tm), lambda bi, i: (bi, 0, 0, i)),
            pl.BlockSpec((1, tm, N_BRANCH * d), lambda bi, i: (bi, i, COL_G // 32)),
            const((4, DH_A)), const((1, 2 * DH_A)),
            const((N_BRANCH, BRANCH_W, d)), const((d, d)), const((1, d)),
        ],
        out_specs=pl.BlockSpec((1, tm, d), lambda bi, i: (bi, i, 0)),
        out_shape=jax.ShapeDtypeStruct((b, l, d), F32),
        compiler_params=_params(("parallel", "parallel")),
        name="merge",
    )(x, oa, ob, yc, od, u, lam_p, subln, wb, wo, post)


def _mem_attn_kernel(x_ref, kv_ref, pre_ref, wq_ref, wo_ref, post_ref, o_ref):
    x = x_ref[0]
    xn = _rms(x, pre_ref[...]).astype(BF16)
    q = jnp.dot(xn, wq_ref[...], preferred_element_type=F32).astype(BF16)
    outs = []
    for h in range(H_MEM):
        k = kv_ref[0, :, 2 * h * DH_MEM:(2 * h + 1) * DH_MEM]
        v = kv_ref[0, :, (2 * h + 1) * DH_MEM:(2 * h + 2) * DH_MEM]
        s = lax.dot_general(q[:, h * DH_MEM:(h + 1) * DH_MEM], k, (((1,), (1,)), ((), ())),
                            preferred_element_type=F32)
        p = jnp.exp2(s - jnp.max(s, axis=-1, keepdims=True))
        o = jnp.dot(p.astype(BF16), v, preferred_element_type=F32)
        outs.append(o / jnp.sum(p, axis=-1, keepdims=True))
    o = jnp.concatenate(outs, axis=1).astype(BF16)
    y = jnp.dot(o, wo_ref[...], preferred_element_type=F32)
    o_ref[0] = x + _rms(y, post_ref[...])


def _mem_attn(x, kv, pre, wq, wo, post, tm):
    b, l, d = x.shape
    m = kv.shape[1]
    const = lambda shape: pl.BlockSpec(shape, lambda bi, i: (0,) * len(shape))
    return pl.pallas_call(
        _mem_attn_kernel,
        grid=(b, l // tm),
        in_specs=[
            pl.BlockSpec((1, tm, d), lambda bi, i: (bi, i, 0)),
            pl.BlockSpec((1, m, 2 * H_MEM * DH_MEM), lambda bi, i: (bi, 0, 0)),
            const((1, d)), const((d, H_MEM * DH_MEM)), const((H_MEM * DH_MEM, d)), const((1, d)),
        ],
        out_specs=pl.BlockSpec((1, tm, d), lambda bi, i: (bi, i, 0)),
        out_shape=jax.ShapeDtypeStruct((b, l, d), F32),
        compiler_params=_params(("parallel", "parallel")),
        name="mem_attn",
    )(x, kv, pre, wq, wo, post)


def _ffn_kernel(p_ref, x_ref, n_ref, pre_ref, wa_ref, wg_ref, cwa_ref, cwg_ref, cba_ref, cbg_ref, wo_ref, post_ref,
                o_ref, xe_ref, acc_ref):
    tm = x_ref.shape[1]
    c = pl.program_id(2)

    @pl.when(c == 0)
    def _():
        xe = _halo_rows(p_ref, x_ref, n_ref, pl.program_id(1), pl.num_programs(1))
        xe_ref[...] = _rms(xe, pre_ref[...]).astype(BF16)
        acc_ref[...] = jnp.zeros(acc_ref.shape, F32)

    xe = xe_ref[...]
    a = _conv3_rows(jnp.dot(xe, wa_ref[...], preferred_element_type=F32), cwa_ref, cba_ref, tm)
    g = _conv3_rows(jnp.dot(xe, wg_ref[...], preferred_element_type=F32), cwg_ref, cbg_ref, tm)
    act = 0.5 * a * (1.0 + lax.erf(a * (2.0 ** -0.5))) * g
    acc_ref[...] += jnp.dot(act.astype(BF16), wo_ref[...], preferred_element_type=F32)

    @pl.when(c == pl.num_programs(2) - 1)
    def _():
        o_ref[0] = x_ref[0] + _rms(acc_ref[...], post_ref[...])


def _ffn(x, pre, w_in, conv_w, conv_b, w_out, post, tm, tf):
    b, l, d = x.shape
    nf = D_FF // tf
    r8 = tm // 8
    n8 = l // 8
    return pl.pallas_call(
        _ffn_kernel,
        grid=(b, l // tm, nf),
        in_specs=[
            pl.BlockSpec((1, 8, d), lambda bi, i, c: (bi, jnp.maximum(i * r8 - 1, 0), 0)),
            pl.BlockSpec((1, tm, d), lambda bi, i, c: (bi, i, 0)),
            pl.BlockSpec((1, 8, d), lambda bi, i, c: (bi, jnp.minimum((i + 1) * r8, n8 - 1), 0)),
            pl.BlockSpec((1, d), lambda bi, i, c: (0, 0)),
            pl.BlockSpec((d, tf), lambda bi, i, c: (0, c)),
            pl.BlockSpec((d, tf), lambda bi, i, c: (0, nf + c)),
            pl.BlockSpec((3, tf), lambda bi, i, c: (0, c)),
            pl.BlockSpec((3, tf), lambda bi, i, c: (0, nf + c)),
            pl.BlockSpec((1, tf), lambda bi, i, c: (0, c)),
            pl.BlockSpec((1, tf), lambda bi, i, c: (0, nf + c)),
            pl.BlockSpec((tf, d), lambda bi, i, c: (c, 0)),
            pl.BlockSpec((1, d), lambda bi, i, c: (0, 0)),
        ],
        out_specs=pl.BlockSpec((1, tm, d), lambda bi, i, c: (bi, i, 0)),
        out_shape=jax.ShapeDtypeStruct((b, l, d), F32),
        scratch_shapes=[pltpu.VMEM((tm + 16, d), BF16), pltpu.VMEM((tm, d), F32)],
        compiler_params=_params(("parallel", "parallel", "arbitrary")),
        name="conv_ffn",
    )(x, x, x, pre, w_in, w_in, conv_w, conv_w, conv_b, conv_b, w_out, post)


def _rot_half_cols(w, half):
    shp = w.shape
    w = w.reshape(shp[:-1] + (shp[-1] // (2 * half), 2, half))
    return jnp.concatenate([-w[..., 1:2, :], w[..., 0:1, :]], axis=-2).reshape(shp)


def _swap_half(g, half):
    shp = g.shape
    g = g.reshape(shp[:-1] + (shp[-1] // (2 * half), 2, half))
    return jnp.concatenate([g[..., 1:2, :], g[..., 0:1, :]], axis=-2).reshape(shp)


def _pack_w_in(w_in):
    sizes = (512, 512, 512, Q_LORA, KV_LORA, ROPE_B, D_INNER, D_INNER + 2 * N_GROUPS * D_STATE, H_C, H_C,
             H_D * DH_D, KV_D * DH_D, KV_D * DH_D, N_BRANCH * D_MODEL)
    offs = np.concatenate([[0], np.cumsum(sizes)])
    (a_q, a_k, a_v, b_cq, b_ckv, b_kr, c_z, c_xbc, c_dtf, c_dtb, d_q, d_k, d_v, g) = [
        w_in[:, offs[i]:offs[i + 1]] for i in range(len(sizes))]
    d = w_in.shape[0]
    z = lambda n: jnp.zeros((d, n), w_in.dtype)
    kr_blk = jnp.concatenate([z(NOPE_B), b_kr, z(LANE - NOPE_B - ROPE_B)], axis=1)
    krr_blk = jnp.concatenate([z(NOPE_B), _rot_half_cols(b_kr, ROPE_B // 2), z(LANE - NOPE_B - ROPE_B)], axis=1)
    dt_blk = jnp.concatenate([c_dtf, c_dtb, z(LANE - 2 * H_C)], axis=1)
    cols = [
        g,
        a_q * (DH_A ** -0.5 * LOG2E), a_k, a_v,
        c_z, c_xbc[:, :D_INNER],
        d_q, _rot_half_cols(d_q, DH_D // 4),
        b_cq, kr_blk, b_ckv, c_xbc[:, D_INNER:],
        krr_blk, dt_blk,
        d_k, _rot_half_cols(d_k, DH_D // 4), d_v,
        z((N_COLBLK - COL_DV - 1) * LANE),
    ]
    return jnp.concatenate(cols, axis=1).astype(BF16)


def _pad_heads(w, n_heads, per_head, take_from, take, place_at):
    d = w.shape[0]
    w = w.reshape(d, n_heads, per_head)[:, :, take_from:take_from + take]
    w = jnp.pad(w, ((0, 0), (0, 0), (place_at, LANE - place_at - take)))
    return w.reshape(d, n_heads * LANE)


def _rope_tables(l):
    half = ROPE_B // 2
    freqs = ROPE_THETA ** (-jnp.arange(half, dtype=F32) / half)
    pos = jnp.arange(l, dtype=F32)

    def cs(p):
        ang = p[:, None] * freqs[None, :]
        return jnp.cos(ang), jnp.sin(ang)

    c, s = cs(pos)
    ones = jnp.ones((l, NOPE_B), F32)
    zeros = jnp.zeros((l, LANE - NOPE_B - ROPE_B), F32)
    cos_b = jnp.concatenate([ones, c, c, zeros], axis=1)
    sin_b = jnp.concatenate([0 * ones, s, s, zeros], axis=1)
    cr, sr = cs(jnp.floor(pos / GRID_W))
    cc, sc = cs(pos - GRID_W * jnp.floor(pos / GRID_W))
    cos_d = jnp.concatenate([cr, cr, cc, cc] * 2, axis=1)
    sin_d = jnp.concatenate([sr, sr, sc, sc] * 2, axis=1)
    return cos_b, sin_b, cos_d, sin_d


def _layer(x, kv_mem, lp, layer_idx, tables):
    b, l, d = x.shape
    cos_b, sin_b, cos_d, sin_d = tables
    tm = min(512, l)
    t = b * l

    u = _norm_matmul(x.reshape(t, d), lp["norm_pre"][0], lp["w_in_packed"], BF16, min(1024, t), PROJ_TN)
    u = u.reshape(b, l, N_COLBLK * LANE)

    tq = min(1024, l)
    tk = tq // 2
    qt_a, k_a, vt_a = _prep_a(u, lp["alibi_slopes"], b, l, tm, tq, tk)
    o_a = _flash(qt_a, k_a, vt_a, lp["alibi"], tq=tq, tk=tk, alibi=True)
    qt_b, k_b, vt_b = _prep_b(u, b, l, tm, cos_b, sin_b, lp["mla_q_norm"].reshape(1, -1),
                              lp["mla_kv_norm"].reshape(1, -1), lp["wq_b"], lp["wqr_b"], lp["wk_b"], lp["wv_b"])
    o_b = _flash(qt_b, k_b, vt_b, lp["alibi"], tq=tq, tk=tk, alibi=False, cb=256, stages=3)
    qt_d, k_d, vt_d = _prep_d(u, b, l, tm, cos_d, sin_d, lp["gq_d"], lp["gqr_d"], lp["gk_d"], lp["gkr_d"])
    o_d = _flash(qt_d, k_d, vt_d, lp["alibi"], tq=tq, tk=tk, alibi=False)

    xs = _conv_silu(u, b, l, tm, COL_CXS // 4, D_INNER, lp["ssm_conv_w"][:, :D_INNER], lp["ssm_conv_b"][None, :D_INNER])
    bc = _conv_silu(u, b, l, tm, COL_CBC // 2, 2 * LANE, lp["ssm_conv_w"][:, D_INNER:], lp["ssm_conv_b"][None, D_INNER:])
    y_f = _ssd_scan(xs, bc, u, b, l, lp["alog_vec"], lp["dtb_vec"], reverse=False)
    y_c = _ssd_scan(xs, bc, u, b, l, lp["alog_vec"], lp["dtb_vec"], reverse=True,
                    final_args=(y_f, lp["dskip_x"], lp["ssm_norm"].reshape(1, -1)))

    lam_init = 0.8 - 0.6 * math.exp(-0.3 * layer_idx)
    x = _merge(x, o_a, o_b, y_c, o_d, u, lp["diff_lambda"], lp["diff_norm"].reshape(1, -1), lp["w_branch16"],
               lp["w_out16"], lp["norm_post"][0].reshape(1, -1), lam_init, min(512, l))

    x = _mem_attn(x, kv_mem, lp["norm_pre"][1].reshape(1, -1), lp["w_mem_q16"], lp["w_mem_o16"],
                  lp["norm_post"][1].reshape(1, -1), tm)

    x = _ffn(x, lp["norm_pre"][2].reshape(1, -1), lp["w_ffn_in16"], lp["ffn_conv_w"], lp["ffn_conv_b"].reshape(1, -1),
             lp["w_ffn_out16"], lp["norm_post"][2].reshape(1, -1), min(1024, l), D_FF // 2)
    return x


def _prep_layer_params(p, layer_idx):
    lp = {name: arr[layer_idx] for name, arr in p.items()}
    lp["w_in_packed"] = _pack_w_in(lp["w_in"])
    qscale_b = (NOPE_B + ROPE_B) ** -0.5 * LOG2E
    wq = lp["w_mla_uq"] * qscale_b
    per_q = NOPE_B + ROPE_B
    lp["wq_b"] = _pad_heads(wq, H_B, per_q, 0, per_q, 0).astype(BF16)
    wq_rope = wq.reshape(Q_LORA, H_B, per_q)[:, :, NOPE_B:]
    lp["wqr_b"] = _pad_heads(_rot_half_cols(wq_rope, ROPE_B // 2).reshape(Q_LORA, H_B * ROPE_B), H_B, ROPE_B, 0, ROPE_B,
                             NOPE_B).astype(BF16)
    per_kv = NOPE_B + V_B
    lp["wk_b"] = _pad_heads(lp["w_mla_ukv"], H_B, per_kv, 0, NOPE_B, 0).astype(BF16)
    lp["wv_b"] = _pad_heads(lp["w_mla_ukv"], H_B, per_kv, NOPE_B, V_B, 0).astype(BF16)
    tile2 = lambda g: jnp.tile(g, 2).reshape(1, LANE)
    lp["gq_d"] = tile2(lp["gqa_q_norm"])
    lp["gqr_d"] = tile2(_swap_half(lp["gqa_q_norm"], DH_D // 4))
    lp["gk_d"] = tile2(lp["gqa_k_norm"])
    lp["gkr_d"] = tile2(_swap_half(lp["gqa_k_norm"], DH_D // 4))
    slopes = np.array([2.0 ** (-8.0 * (i + 1) / H_A) for i in range(H_A)], np.float32)
    lp["alibi_slopes"] = jnp.asarray(slopes)
    lp["alibi"] = jnp.asarray(np.repeat(slopes * np.float32(LOG2E), 2))
    pad_vec = lambda v: jnp.pad(v.reshape(-1), (0, LANE - 2 * H_C)).reshape(1, LANE)
    lp["alog_vec"] = pad_vec(lp["ssm_a_log"])
    lp["dtb_vec"] = pad_vec(lp["ssm_dt_bias"])
    lp["dskip_x"] = jnp.repeat(lp["ssm_d"], SSM_HEAD).reshape(1, D_INNER)
    lp["w_branch16"] = lp["w_branch"].astype(BF16)
    lp["w_out16"] = lp["w_out"].astype(BF16)
    lp["w_mem_q16"] = (lp["w_mem_q"] * (DH_MEM ** -0.5 * LOG2E)).astype(BF16)
    lp["w_mem_kv16"] = lp["w_mem_kv"].astype(BF16)
    lp["w_mem_o16"] = lp["w_mem_o"].astype(BF16)
    lp["w_ffn_in16"] = lp["w_ffn_in"].astype(BF16)
    lp["w_ffn_out16"] = lp["w_ffn_out"].astype(BF16)
    return lp


def _trunk(x, mem, layer_params):
    b, l, d = x.shape
    m = mem.shape[1]
    tables = _rope_tables(l)
    for layer_idx, lp in enumerate(layer_params):
        kv = _norm_matmul(mem.reshape(b * m, d), lp["mem_norm"], lp["w_mem_kv16"], BF16, min(256, b * m), 512)
        x = _layer(x, kv.reshape(b, m, 2 * H_MEM * DH_MEM), lp, layer_idx, tables)
    return x


def kernel(x_prompt, x_sample, mem_prompt, mem_sample, w_in, w_branch, w_out, diff_lambda, diff_norm, mla_q_norm,
           mla_kv_norm, w_mla_uq, w_mla_ukv, ssm_conv_w, ssm_conv_b, ssm_a_log, ssm_dt_bias, ssm_d, ssm_norm,
           gqa_q_norm, gqa_k_norm, mem_norm, w_mem_q, w_mem_kv, w_mem_o, w_ffn_in, ffn_conv_w, ffn_conv_b, w_ffn_out,
           norm_pre, norm_post):
    p = {
        "w_in": w_in, "w_branch": w_branch, "w_out": w_out, "diff_lambda": diff_lambda, "diff_norm": diff_norm,
        "mla_q_norm": mla_q_norm, "mla_kv_norm": mla_kv_norm, "w_mla_uq": w_mla_uq, "w_mla_ukv": w_mla_ukv,
        "ssm_conv_w": ssm_conv_w, "ssm_conv_b": ssm_conv_b, "ssm_a_log": ssm_a_log, "ssm_dt_bias": ssm_dt_bias,
        "ssm_d": ssm_d, "ssm_norm": ssm_norm, "gqa_q_norm": gqa_q_norm, "gqa_k_norm": gqa_k_norm,
        "mem_norm": mem_norm, "w_mem_q": w_mem_q, "w_mem_kv": w_mem_kv, "w_mem_o": w_mem_o,
        "w_ffn_in": w_ffn_in, "ffn_conv_w": ffn_conv_w, "ffn_conv_b": ffn_conv_b, "w_ffn_out": w_ffn_out,
        "norm_pre": norm_pre, "norm_post": norm_post,
    }
    layer_params = [_prep_layer_params(p, i) for i in range(DEPTH)]
    y_prompt = _trunk(x_prompt, mem_prompt, layer_params)
    y_sample = _trunk(x_sample, mem_sample, layer_params)
    return (y_prompt, y_sample)
```
